```python
import jax
import jax.numpy as jnp
from jax import lax
import numpy as np

D_MODEL = 4096
BATCH = 2
SEQ = 4096
DEPTH = 2

CHUNK = 64
RET_HEADS = 4
RET_DK = 128
RET_DV = 256
GLA_HEADS = 4
GLA_DK = 128
GLA_DV = 256
GLA_RANK = 16
GLA_TAU = 16.0
DN_HEADS = 16
DN_DK = 128
DN_DV = 128
CONV_K = 4

ROPE_BASE = 10000.0
EPS = 1e-6
LN_EPS = 1e-5
DEEPNORM_ALPHA = float((2 * DEPTH) ** 0.25)
DEEPNORM_BETA = float((8 * DEPTH) ** -0.25)

D_MIX = RET_HEADS * RET_DV + GLA_HEADS * GLA_DV + DN_HEADS * DN_DV
DN_QKV = 2 * DN_HEADS * DN_DK + DN_HEADS * DN_DV
SPLITS = (RET_HEADS * RET_DK, RET_HEADS * RET_DK, RET_HEADS * RET_DV, RET_HEADS * RET_DV,
          GLA_HEADS * GLA_DK, GLA_HEADS * GLA_DK, GLA_HEADS * GLA_DV, GLA_HEADS * GLA_DV, GLA_RANK,
          DN_QKV, DN_HEADS * DN_DV, DN_HEADS, DN_HEADS)
SPLIT_POINTS = tuple(int(v) for v in np.cumsum(SPLITS)[:-1])
D_IN = int(sum(SPLITS))

kernel_name = 'hybrid_retention_gla_gdn_deepnorm'


def to_chunks(t):
    b, s, h, d = t.shape
    return t.reshape(b, s // CHUNK, CHUNK, h, d).transpose(1, 0, 3, 2, 4)


def from_chunks(t):
    n, b, h, c, d = t.shape
    return t.transpose(1, 0, 3, 2, 4).reshape(b, n * c, h, d)


def rotary(t, positions):
    half = t.shape[-1] // 2
    inv = ROPE_BASE ** (-jnp.arange(half, dtype=jnp.float32) / half)
    ang = positions.astype(jnp.float32)[..., None] * inv
    cos = jnp.cos(ang)[:, :, None, :]
    sin = jnp.sin(ang)[:, :, None, :]
    t1, t2 = t[..., :half], t[..., half:]
    return jnp.concatenate([t1 * cos - t2 * sin, t1 * sin + t2 * cos], axis=-1)


def head_layernorm(t):
    mu = jnp.mean(t, axis=-1, keepdims=True)
    var = jnp.mean(jnp.square(t - mu), axis=-1, keepdims=True)
    return (t - mu) * lax.rsqrt(var + EPS)


def head_rmsnorm(t):
    return t * lax.rsqrt(jnp.mean(jnp.square(t), axis=-1, keepdims=True) + EPS)


def l2norm(t):
    return t * lax.rsqrt(jnp.sum(jnp.square(t), axis=-1, keepdims=True) + EPS)


def layernorm(t, g, b):
    t = t.astype(jnp.float32)
    mu = jnp.mean(t, axis=-1, keepdims=True)
    var = jnp.mean(jnp.square(t - mu), axis=-1, keepdims=True)
    return (t - mu) * lax.rsqrt(var + LN_EPS) * g + b


def causal_depthwise_conv(u, w):
    taps, t = w.shape[0], u.shape[1]
    up = jnp.pad(u, ((0, 0), (taps - 1, 0), (0, 0)))
    out = up[:, 0:t] * w[0]
    for i in range(1, taps):
        out = out + up[:, i:i + t] * w[i]
    return out


def retention_chunkwise(q, k, v):
    b, _, h, dk = q.shape
    dv = v.shape[-1]
    log_gamma = jnp.log(1.0 - jnp.power(2.0, -5.0 - jnp.arange(h, dtype=jnp.float32)))
    i = jnp.arange(CHUNK, dtype=jnp.float32)
    diff = i[:, None] - i[None, :]
    causal = diff >= 0
    dmask = jnp.where(causal, jnp.exp(jnp.where(causal, diff, 0.0)[None] * log_gamma[:, None, None]), 0.0)
    xi = jnp.exp((i[None, :] + 1.0) * log_gamma[:, None])[..., None]
    zeta = jnp.exp((CHUNK - 1.0 - i[None, :]) * log_gamma[:, None])[..., None]
    g_chunk = jnp.exp(CHUNK * log_gamma)[:, None, None]

    def step(S, inp):
        q_c, k_c, v_c = inp
        scores = jnp.einsum('bhik,bhjk->bhij', q_c, k_c) * dmask
        o = jnp.einsum('bhij,bhjv->bhiv', scores, v_c) + jnp.einsum('bhik,bhkv->bhiv', q_c, S) * xi
        S = g_chunk * S + jnp.einsum('bhjk,bhjv->bhkv', k_c * zeta, v_c)
        return S, o

    S0 = jnp.zeros((b, h, dk, dv), jnp.float32)
    _, o = lax.scan(step, S0, (to_chunks(q), to_chunks(k), to_chunks(v)))
    return from_chunks(o)


def gla_chunkwise(q, k, v, log_a):
    b, _, h, dk = q.shape
    dv = v.shape[-1]
    causal = jnp.tril(jnp.ones((CHUNK, CHUNK), dtype=bool))

    def step(S, inp):
        q_c, k_c, v_c, a_c = inp
        b_c = jnp.cumsum(a_c, axis=2)
        b_last = b_c[:, :, -1:, :]
        diff = b_c[:, :, :, None, :] - b_c[:, :, None, :, :]
        decay = jnp.exp(jnp.where(causal[:, :, None], diff, -jnp.inf))
        scores = jnp.einsum('bhik,bhjk,bhijk->bhij', q_c, k_c, decay)
        o = jnp.einsum('bhij,bhjv->bhiv', scores, v_c) + jnp.einsum('bhik,bhkv->bhiv', q_c * jnp.exp(b_c), S)
        S = jnp.exp(b_last[:, :, 0, :])[..., None] * S + jnp.einsum('bhjk,bhjv->bhkv', k_c * jnp.exp(b_last - b_c), v_c)
        return S, o

    S0 = jnp.zeros((b, h, dk, dv), jnp.float32)
    _, o = lax.scan(step, S0, (to_chunks(q), to_chunks(k), to_chunks(v), to_chunks(log_a)))
    return from_chunks(o)


def gated_delta_chunkwise(q, k, v, beta, g):
    b, _, h, dk = q.shape
    dv = v.shape[-1]
    qc, kc, vc = to_chunks(q), to_chunks(k), to_chunks(v)
    bc = to_chunks(beta[..., None])[..., 0]
    gc = jnp.cumsum(to_chunks(g[..., None])[..., 0], axis=-1)
    lower = jnp.tril(jnp.ones((CHUNK, CHUNK), dtype=bool))
    strict = jnp.tril(jnp.ones((CHUNK, CHUNK), dtype=bool), k=-1)
    decay = jnp.exp(jnp.where(lower, gc[..., :, None] - gc[..., None, :], -jnp.inf))
    kk = jnp.einsum('nbhik,nbhjk->nbhij', kc, kc)
    a_strict = jnp.where(strict, bc[..., :, None] * kk * decay, 0.0)
    m = a_strict + jnp.eye(CHUNK, dtype=jnp.float32)
    u = lax.linalg.triangular_solve(m, vc * bc[..., None], left_side=True, lower=True, unit_diagonal=True)
    w = lax.linalg.triangular_solve(m, kc * (bc * jnp.exp(gc))[..., None], left_side=True, lower=True, unit_diagonal=True)
    qk = jnp.einsum('nbhik,nbhjk->nbhij', qc, kc) * decay

    def step(S, inp):
        q_c, k_c, u_c, w_c, g_c, qk_c = inp
        v_new = u_c - jnp.einsum('bhik,bhkv->bhiv', w_c, S)
        o = jnp.einsum('bhik,bhkv->bhiv', q_c * jnp.exp(g_c)[..., None], S) + jnp.einsum('bhij,bhjv->bhiv', qk_c, v_new)
        g_last = g_c[..., -1:]
        S = jnp.exp(g_last)[..., None] * S + jnp.einsum('bhjk,bhjv->bhkv', k_c * jnp.exp(g_last - g_c)[..., None], v_new)
        return S, o

    S0 = jnp.zeros((b, h, dk, dv), jnp.float32)
    _, o = lax.scan(step, S0, (qc, kc, u, w, gc, qk))
    return from_chunks(o)


def hybrid_layer(x, positions, w_in, gla_w_a2, gla_b_a, dn_conv_w, dn_a_log, dn_dt_bias,
                 ret_norm_g, gla_norm_g, dn_norm_g, w_out, ln_g, ln_b):
    b, s, _ = x.shape
    proj = jnp.einsum('btd,de->bte', x, w_in).astype(jnp.float32)
    (rq, rk, rv, rg, gq, gk, gv, gg, ga1, dqkv, dg, dbeta, da) = jnp.split(proj, SPLIT_POINTS, axis=-1)

    rq = rotary(rq.reshape(b, s, RET_HEADS, RET_DK), positions) * RET_DK ** -0.5
    rk = rotary(rk.reshape(b, s, RET_HEADS, RET_DK), positions)
    ro = retention_chunkwise(rq, rk, rv.reshape(b, s, RET_HEADS, RET_DV))
    ro = head_layernorm(ro) * ret_norm_g.reshape(RET_HEADS, RET_DV)
    ro = ro.reshape(b, s, -1) * jax.nn.silu(rg)

    log_a = jax.nn.log_sigmoid(jnp.einsum('btr,rk->btk', ga1, gla_w_a2) + gla_b_a) / GLA_TAU
    go = gla_chunkwise(gq.reshape(b, s, GLA_HEADS, GLA_DK) * GLA_DK ** -0.5,
                       gk.reshape(b, s, GLA_HEADS, GLA_DK),
                       gv.reshape(b, s, GLA_HEADS, GLA_DV),
                       log_a.reshape(b, s, GLA_HEADS, GLA_DK))
    go = head_rmsnorm(go) * gla_norm_g.reshape(GLA_HEADS, GLA_DV)
    go = go.reshape(b, s, -1) * jax.nn.silu(gg)

    dqkv = jax.nn.silu(causal_depthwise_conv(dqkv, dn_conv_w))
    dq, dk, dv = jnp.split(dqkv, (DN_HEADS * DN_DK, 2 * DN_HEADS * DN_DK), axis=-1)
    dq = l2norm(dq.reshape(b, s, DN_HEADS, DN_DK)) * DN_DK ** -0.5
    dk = l2norm(dk.reshape(b, s, DN_HEADS, DN_DK))
    beta = jax.nn.sigmoid(dbeta)
    g = -jnp.exp(dn_a_log.astype(jnp.float32)) * jax.nn.softplus(da + dn_dt_bias)
    do = gated_delta_chunkwise(dq, dk, dv.reshape(b, s, DN_HEADS, DN_DV), beta, g)
    do = head_rmsnorm(do) * dn_norm_g.reshape(DN_HEADS, DN_DV)
    do = do.reshape(b, s, -1) * jax.nn.silu(dg)

    mix = jnp.concatenate([ro, go, do], axis=-1).astype(x.dtype)
    y = jnp.einsum('bte,ed->btd', mix, w_out)
    return layernorm(DEEPNORM_ALPHA * x + y, ln_g, ln_b).astype(x.dtype)


def setup_inputs(seed: int = 0) -> dict:
    key = jax.random.key(seed)
    ks = jax.random.split(key, 16)
    x = jax.random.normal(ks[0], (BATCH, SEQ, D_MODEL), jnp.float32)
    positions = (jnp.arange(SEQ, dtype=jnp.int32)[None, :]
                 + jax.random.randint(ks[1], (BATCH, 1), 0, 1024, dtype=jnp.int32))
    starts = np.concatenate([[0], np.cumsum(SPLITS)]).astype(np.int64)
    col_scale = np.ones((D_IN,), np.float32)
    col_scale[starts[2]:starts[3]] = DEEPNORM_BETA
    col_scale[starts[6]:starts[7]] = DEEPNORM_BETA
    col_scale[starts[10] - DN_HEADS * DN_DV:starts[10]] = DEEPNORM_BETA
    w_in = (jax.random.normal(ks[2], (DEPTH, D_MODEL, D_IN), jnp.float32)
            * (D_MODEL ** -0.5) * jnp.asarray(col_scale))
    gla_w_a2 = jax.random.normal(ks[3], (DEPTH, GLA_RANK, GLA_HEADS * GLA_DK), jnp.float32) * GLA_RANK ** -0.5
    gla_b_a = 0.1 * jax.random.normal(ks[4], (DEPTH, GLA_HEADS * GLA_DK), jnp.float32)
    dn_conv_w = jax.random.normal(ks[5], (DEPTH, CONV_K, DN_QKV), jnp.float32) * CONV_K ** -0.5
    dn_a_log = jnp.log(jax.random.uniform(ks[6], (DEPTH, DN_HEADS), jnp.float32, 1.0, 16.0))
    dt = jnp.exp(jax.random.uniform(ks[7], (DEPTH, DN_HEADS), jnp.float32,
                                    float(np.log(1e-3)), float(np.log(1e-1))))
    dn_dt_bias = dt + jnp.log(-jnp.expm1(-dt))
    ret_norm_g = 1.0 + 0.02 * jax.random.normal(ks[8], (DEPTH, RET_HEADS * RET_DV), jnp.float32)
    gla_norm_g = 1.0 + 0.02 * jax.random.normal(ks[9], (DEPTH, GLA_HEADS * GLA_DV), jnp.float32)
    dn_norm_g = 1.0 + 0.02 * jax.random.normal(ks[10], (DEPTH, DN_HEADS * DN_DV), jnp.float32)
    w_out = (jax.random.normal(ks[11], (DEPTH, D_MIX, D_MODEL), jnp.float32)
             * (D_MIX ** -0.5) * DEEPNORM_BETA)
    ln_g = 1.0 + 0.02 * jax.random.normal(ks[12], (DEPTH, D_MODEL), jnp.float32)
    ln_b = 0.02 * jax.random.normal(ks[13], (DEPTH, D_MODEL), jnp.float32)
    return {'x': x, 'positions': positions, 'w_in': w_in, 'gla_w_a2': gla_w_a2, 'gla_b_a': gla_b_a,
            'dn_conv_w': dn_conv_w, 'dn_a_log': dn_a_log, 'dn_dt_bias': dn_dt_bias,
            'ret_norm_g': ret_norm_g, 'gla_norm_g': gla_norm_g, 'dn_norm_g': dn_norm_g,
            'w_out': w_out, 'ln_g': ln_g, 'ln_b': ln_b}


def reference(x, positions, w_in, gla_w_a2, gla_b_a, dn_conv_w, dn_a_log, dn_dt_bias,
              ret_norm_g, gla_norm_g, dn_norm_g, w_out, ln_g, ln_b):
    h = x
    for l in range(DEPTH):
        h = hybrid_layer(h, positions, w_in[l], gla_w_a2[l], gla_b_a[l], dn_conv_w[l], dn_a_log[l],
                         dn_dt_bias[l], ret_norm_g[l], gla_norm_g[l], dn_norm_g[l], w_out[l],
                         ln_g[l], ln_b[l])
    return h
```

```python
import functools

import jax
import jax.numpy as jnp
import numpy as np
from jax import lax
from jax.experimental import pallas as pl
from jax.experimental.pallas import tpu as pltpu

CHUNK = 64
RET_HEADS, RET_DK, RET_DV = 4, 128, 256
GLA_HEADS, GLA_DK, GLA_DV, GLA_RANK, GLA_TAU = 4, 128, 256, 16, 16.0
DN_HEADS, DN_DK, DN_DV, CONV_K = 16, 128, 128, 4
ROPE_BASE = 10000.0
EPS = 1e-6
LN_EPS = 1e-5

LANE = 128
SUB = 16

CB_RQ, CB_RK, CB_RV, CB_RG = 0, 4, 8, 16
CB_GQ, CB_GK, CB_GV, CB_GG = 24, 28, 32, 40
CB_DQ, CB_DK, CB_DV, CB_DG = 48, 64, 80, 96
N_CB = 112
N_MAIN = N_CB * LANE
SM_GA, SM_BETA, SM_A = 0, 16, 32
D_MIX = 4096

VMEM_LIMIT = 56 * 1024 * 1024

F32 = jnp.float32
BF16 = jnp.bfloat16
HI = lax.Precision.HIGHEST


def _mm(a, b):
    return jnp.dot(a.astype(BF16), b.astype(BF16), preferred_element_type=F32)


def _mm_nt(a, b):
    return lax.dot_general(a.astype(BF16), b.astype(BF16), (((1,), (1,)), ((), ())),
                           preferred_element_type=F32)


def _mm_tn(a, b):
    return lax.dot_general(a.astype(BF16), b.astype(BF16), (((0,), (0,)), ((), ())),
                           preferred_element_type=F32)


def _silu(x):
    return x * (1.0 / (1.0 + jnp.exp(-x)))


def _softplus(x):
    return jnp.maximum(x, 0.0) + jnp.log1p(jnp.exp(-jnp.abs(x)))


def _chunk_cumsum_matrix(n):
    ii = lax.broadcasted_iota(jnp.int32, (n, n), 0)
    jj = lax.broadcasted_iota(jnp.int32, (n, n), 1)
    return jnp.where((ii >= jj) & ((ii // CHUNK) == (jj // CHUNK)), 1.0, 0.0).astype(F32)


def _inproj_kernel(x_ref, w_ref, ws_ref, p_ref, s_ref, *, tn):
    x = x_ref[...]
    acc = jnp.dot(x, w_ref[...], preferred_element_type=F32)
    for c in range(tn // LANE):
        p_ref[c] = acc[:, c * LANE:(c + 1) * LANE]

    @pl.when(pl.program_id(1) == 0)
    def _():
        s_ref[...] = jnp.dot(x, ws_ref[...], preferred_element_type=F32)


def _inproj(xb, wcat, wsmall, *, tm, tn):
    m, d = xb.shape
    grid = (m // tm, N_MAIN // tn)
    return pl.pallas_call(
        functools.partial(_inproj_kernel, tn=tn),
        grid=grid,
        in_specs=[
            pl.BlockSpec((tm, d), lambda i, j: (i, 0)),
            pl.BlockSpec((d, tn), lambda i, j: (0, j)),
            pl.BlockSpec((d, LANE), lambda i, j: (0, 0)),
        ],
        out_specs=[
            pl.BlockSpec((tn // LANE, tm, LANE), lambda i, j: (j, i, 0)),
            pl.BlockSpec((tm, LANE), lambda i, j: (i, 0)),
        ],
        out_shape=[
            jax.ShapeDtypeStruct((N_CB, m, LANE), F32),
            jax.ShapeDtypeStruct((m, LANE), F32),
        ],
        compiler_params=pltpu.CompilerParams(
            dimension_semantics=("parallel", "arbitrary"), vmem_limit_bytes=VMEM_LIMIT),
        name="inproj",
    )(xb, wcat, wsmall)


def _ret_kernel(lg_ref, q_ref, k_ref, v_ref, g_ref, pos_ref, inv_ref, ng_ref, o_ref, s_ref, *, tb):
    h = pl.program_id(1)

    @pl.when(pl.program_id(2) == 0)
    def _():
        s_ref[...] = jnp.zeros_like(s_ref)

    lg = lg_ref[h]
    ang = pos_ref[...].astype(F32) * inv_ref[...]
    lane = lax.broadcasted_iota(jnp.int32, (1, LANE), 1)
    cosv = jnp.cos(ang)
    sinv = jnp.sin(ang) * jnp.where(lane < LANE // 2, -1.0, 1.0)

    def rot(t):
        return t * cosv + pltpu.roll(t, LANE // 2, 1) * sinv

    q = rot(q_ref[...]) * (RET_DK ** -0.5)
    k = rot(k_ref[...])
    v = jnp.concatenate([v_ref[0], v_ref[1]], axis=1)

    ii = lax.broadcasted_iota(jnp.int32, (tb, tb), 0)
    jj = lax.broadcasted_iota(jnp.int32, (tb, tb), 1)
    causal = ii >= jj
    dmask = jnp.where(causal, jnp.exp(jnp.where(causal, (ii - jj).astype(F32), 0.0) * lg), 0.0)
    ri = lax.broadcasted_iota(jnp.int32, (tb, RET_DV), 0).astype(F32)
    xi = jnp.exp((ri + 1.0) * lg)
    zeta = jnp.exp((tb - 1.0 - ri[:, :RET_DK]) * lg)
    g_chunk = jnp.exp(jnp.full((1, RET_DV), tb * lg, F32))

    s = s_ref[...]
    scores = _mm_nt(q, k) * dmask
    o = _mm(scores, v) + _mm(q, s) * xi
    s_ref[...] = g_chunk * s + _mm_tn(k * zeta, v)

    mu = jnp.mean(o, axis=-1, keepdims=True)
    d = o - mu
    var = jnp.mean(d * d, axis=-1, keepdims=True)
    y = d * lax.rsqrt(var + EPS) * ng_ref[...]
    gate = jnp.concatenate([g_ref[0], g_ref[1]], axis=1)
    o_ref[...] = (y * _silu(gate)).astype(o_ref.dtype)


def _retention(proj, pos_col, inv2, lg, ng, *, b, t, tb):
    m = b * t
    nt = t // tb
    row = lambda bb, hh, tt: bb * nt + tt
    return pl.pallas_call(
        functools.partial(_ret_kernel, tb=tb),
        grid=(b, RET_HEADS, nt),
        in_specs=[
            pl.BlockSpec(memory_space=pltpu.SMEM),
            pl.BlockSpec((None, tb, LANE), lambda bb, hh, tt: (CB_RQ + hh, row(bb, hh, tt), 0)),
            pl.BlockSpec((None, tb, LANE), lambda bb, hh, tt: (CB_RK + hh, row(bb, hh, tt), 0)),
            pl.BlockSpec((2, tb, LANE), lambda bb, hh, tt: (CB_RV // 2 + hh, row(bb, hh, tt), 0)),
            pl.BlockSpec((2, tb, LANE), lambda bb, hh, tt: (CB_RG // 2 + hh, row(bb, hh, tt), 0)),
            pl.BlockSpec((tb, 1), lambda bb, hh, tt: (row(bb, hh, tt), 0)),
            pl.BlockSpec((1, LANE), lambda bb, hh, tt: (0, 0)),
            pl.BlockSpec((1, RET_DV), lambda bb, hh, tt: (0, hh)),
        ],
        out_specs=pl.BlockSpec((tb, RET_DV), lambda bb, hh, tt: (row(bb, hh, tt), hh)),
        out_shape=jax.ShapeDtypeStruct((m, D_MIX), BF16),
        scratch_shapes=[pltpu.VMEM((RET_DK, RET_DV), F32)],
        compiler_params=pltpu.CompilerParams(
            dimension_semantics=("parallel", "parallel", "arbitrary"), vmem_limit_bytes=VMEM_LIMIT),
        name="retention",
    )(lg, proj, proj, proj, proj, pos_col, inv2, ng)


def _gla_kernel(q_ref, k_ref, v_ref, g_ref, sm_ref, w2_ref, ba_ref, ng_ref, sel_ref, mix_in_ref,
                o_ref, st_ref, b_ref, p_ref, *, tb):
    del mix_in_ref

    @pl.when(pl.program_id(2) == 0)
    def _():
        st_ref[...] = jnp.zeros_like(st_ref)

    z = jnp.dot(sm_ref[...], w2_ref[...], precision=HI, preferred_element_type=F32) + ba_ref[...]
    log_a = (jnp.minimum(z, 0.0) - jnp.log1p(jnp.exp(-jnp.abs(z)))) / GLA_TAU
    b_ref[...] = jnp.dot(_chunk_cumsum_matrix(tb), log_a, precision=HI, preferred_element_type=F32)

    rid = lax.broadcasted_iota(jnp.int32, (SUB, LANE), 0)
    ci = lax.broadcasted_iota(jnp.int32, (CHUNK, CHUNK), 0)
    cj = lax.broadcasted_iota(jnp.int32, (CHUNK, CHUNK), 1)
    same_sub = (ci // SUB) == (cj // SUB)
    nsub = CHUNK // SUB
    ng = ng_ref[...]
    sel = sel_ref[...]

    for c in range(tb // CHUNK):
        r0 = c * CHUNK
        qc = q_ref[r0:r0 + CHUNK, :] * (GLA_DK ** -0.5)
        kc = k_ref[r0:r0 + CHUNK, :]
        bc = b_ref[r0:r0 + CHUNK, :]
        vc = jnp.concatenate([v_ref[0, r0:r0 + CHUNK, :], v_ref[1, r0:r0 + CHUNK, :]], axis=1)

        for sb in range(nsub):
            qs = qc[sb * SUB:(sb + 1) * SUB]
            bs = bc[sb * SUB:(sb + 1) * SUB]
            for j in range(SUB):
                r = r0 + sb * SUB + j
                e = jnp.exp(jnp.where(rid >= j, bs - b_ref[r:r + 1, :], -jnp.inf))
                p_ref[sb * SUB:(sb + 1) * SUB, j * LANE:(j + 1) * LANE] = (
                    qs * k_ref[r:r + 1, :] * e).astype(BF16)
        diag = jnp.dot(p_ref[...], sel, preferred_element_type=F32)
        scores = jnp.where(same_sub, diag, 0.0)

        qparts, kparts = [], []
        for sb in range(1, nsub):
            lo = sb * SUB
            bref = bc[lo - 1:lo, :]
            qt = qc[lo:lo + SUB] * jnp.exp(bc[lo:lo + SUB] - bref)
            kt = kc[:lo] * jnp.exp(bref - bc[:lo])
            qparts.append(jnp.concatenate(
                [jnp.zeros((lo, LANE), F32), qt] +
                ([jnp.zeros((CHUNK - lo - SUB, LANE), F32)] if CHUNK - lo - SUB else []), axis=0))
            kparts.append(jnp.concatenate([kt, jnp.zeros((CHUNK - lo, LANE), F32)], axis=0))
        scores = scores + _mm_nt(jnp.concatenate(qparts, axis=1), jnp.concatenate(kparts, axis=1))

        st = st_ref[...]
        o = _mm(scores, vc) + _mm_nt(qc * jnp.exp(bc), st)
        b_last = bc[CHUNK - 1:CHUNK, :]
        st_ref[...] = st * jnp.exp(b_last) + _mm_tn(vc, kc * jnp.exp(b_last - bc))

        y = o * lax.rsqrt(jnp.mean(o * o, axis=-1, keepdims=True) + EPS) * ng
        gate = jnp.concatenate([g_ref[0, r0:r0 + CHUNK, :], g_ref[1, r0:r0 + CHUNK, :]], axis=1)
        o_ref[r0:r0 + CHUNK, :] = (y * _silu(gate)).astype(o_ref.dtype)


def _gla(proj, small, w2pad, ba, ng, sel, mix, *, b, t, tb):
    m = b * t
    nt = t // tb
    row = lambda bb, hh, tt: bb * nt + tt
    return pl.pallas_call(
        functools.partial(_gla_kernel, tb=tb),
        grid=(b, GLA_HEADS, nt),
        in_specs=[
            pl.BlockSpec((None, tb, LANE), lambda bb, hh, tt: (CB_GQ + hh, row(bb, hh, tt), 0)),
            pl.BlockSpec((None, tb, LANE), lambda bb, hh, tt: (CB_GK + hh, row(bb, hh, tt), 0)),
            pl.BlockSpec((2, tb, LANE), lambda bb, hh, tt: (CB_GV // 2 + hh, row(bb, hh, tt), 0)),
            pl.BlockSpec((2, tb, LANE), lambda bb, hh, tt: (CB_GG // 2 + hh, row(bb, hh, tt), 0)),
            pl.BlockSpec((tb, LANE), lambda bb, hh, tt: (row(bb, hh, tt), 0)),
            pl.BlockSpec((LANE, GLA_DK), lambda bb, hh, tt: (0, hh)),
            pl.BlockSpec((1, GLA_DK), lambda bb, hh, tt: (0, hh)),
            pl.BlockSpec((1, GLA_DV), lambda bb, hh, tt: (0, hh)),
            pl.BlockSpec((SUB * LANE, CHUNK), lambda bb, hh, tt: (0, 0)),
            pl.BlockSpec(memory_space=pl.ANY),
        ],
        out_specs=pl.BlockSpec((tb, GLA_DV), lambda bb, hh, tt: (row(bb, hh, tt), RET_HEADS + hh)),
        out_shape=jax.ShapeDtypeStruct((m, D_MIX), BF16),
        scratch_shapes=[
            pltpu.VMEM((GLA_DV, GLA_DK), F32),
            pltpu.VMEM((tb, GLA_DK), F32),
            pltpu.VMEM((CHUNK, SUB * LANE), BF16),
        ],
        input_output_aliases={9: 0},
        compiler_params=pltpu.CompilerParams(
            dimension_semantics=("parallel", "parallel", "arbitrary"), vmem_limit_bytes=VMEM_LIMIT),
        name="gla",
    )(proj, proj, proj, proj, small, w2pad, ba, ng, sel, mix)


def _dn_kernel(alog_ref, dtb_ref, q_ref, k_ref, v_ref, g_ref, sm_ref, wq_ref, wk_ref, wv_ref, ng_ref,
               mix_in_ref, o_ref, s_ref, ext_ref, *, tb):
    del mix_in_ref
    h = pl.program_id(1)
    nc = tb // CHUNK

    @pl.when(pl.program_id(2) == 0)
    def _():
        s_ref[...] = jnp.zeros_like(s_ref)
        ext_ref[:, 0:8, :] = jnp.zeros((3, 8, LANE), F32)

    def conv(i, raw_ref, w_ref):
        ext_ref[i, 8:8 + tb, :] = raw_ref[...]
        w = w_ref[...]
        out = ext_ref[i, 8 - (CONV_K - 1):8 - (CONV_K - 1) + tb, :] * w[0:1, :]
        for tap in range(1, CONV_K):
            lo = 8 - (CONV_K - 1) + tap
            out = out + ext_ref[i, lo:lo + tb, :] * w[tap:tap + 1, :]
        ext_ref[i, 0:8, :] = ext_ref[i, tb:tb + 8, :]
        return _silu(out)

    q = conv(0, q_ref, wq_ref)
    k = conv(1, k_ref, wk_ref)
    v = conv(2, v_ref, wv_ref)
    q = q * lax.rsqrt(jnp.sum(q * q, axis=-1, keepdims=True) + EPS) * (DN_DK ** -0.5)
    k = k * lax.rsqrt(jnp.sum(k * k, axis=-1, keepdims=True) + EPS)

    sm = sm_ref[...]
    kk_i = lax.broadcasted_iota(jnp.int32, (LANE, LANE), 0)
    pick_beta = jnp.where(kk_i == SM_BETA + h, 1.0, 0.0).astype(F32)
    pick_a = jnp.where(kk_i == SM_A + h, 1.0, 0.0).astype(F32)
    beta = 1.0 / (1.0 + jnp.exp(-jnp.dot(sm, pick_beta, precision=HI, preferred_element_type=F32)))
    da = jnp.dot(sm, pick_a, precision=HI, preferred_element_type=F32)
    neg_a = -jnp.exp(jnp.full((1, LANE), alog_ref[h], F32))
    g = neg_a * _softplus(da + dtb_ref[h])
    gc = jnp.dot(_chunk_cumsum_matrix(tb), g, precision=HI, preferred_element_type=F32)

    lane = lax.broadcasted_iota(jnp.int32, (1, LANE), 1)
    e0 = jnp.where(lane == 0, 1.0, 0.0).astype(F32)
    e1 = jnp.where(lane == 1, 1.0, 0.0).astype(F32)
    ci = lax.broadcasted_iota(jnp.int32, (CHUNK, CHUNK), 0)
    cj = lax.broadcasted_iota(jnp.int32, (CHUNK, CHUNK), 1)
    lower = ci >= cj
    strict = ci > cj
    eye = jnp.where(ci == cj, 1.0, 0.0).astype(F32)

    egc = jnp.exp(gc)
    rhs_all = jnp.concatenate([v * beta, k * (beta * egc)], axis=1)
    qe_all = q * egc

    s = s_ref[...]
    for c in range(nc):
        r0 = c * CHUNK
        qc, kc = q[r0:r0 + CHUNK], k[r0:r0 + CHUNK]
        gcc = gc[r0:r0 + CHUNK]
        diff = lax.dot_general(gcc * e0 + e1, e0 - gcc * e1, (((1,), (1,)), ((), ())),
                               precision=HI, preferred_element_type=F32)
        decay = jnp.where(lower, jnp.exp(jnp.where(lower, diff, 0.0)), 0.0)
        a = jnp.where(strict, beta[r0:r0 + CHUNK, :CHUNK] * _mm_nt(kc, kc) * decay, 0.0)
        tinv = eye - a
        x = _mm(a, a)
        for _ in range(4):
            tinv = tinv + _mm(tinv, x)
            x = _mm(x, x)
        tinv = tinv + _mm(tinv, x)
        uw = _mm(tinv, rhs_all[r0:r0 + CHUNK])
        u, w = uw[:, :DN_DV], uw[:, DN_DV:]
        qk = _mm_nt(qc, kc) * decay

        v_new = u - _mm(w, s)
        o = _mm(qe_all[r0:r0 + CHUNK], s) + _mm(qk, v_new)
        g_last = gcc[CHUNK - 1:CHUNK, :]
        s = jnp.exp(g_last) * s + _mm_tn(kc * jnp.exp(g_last - gcc), v_new)

        y = o * lax.rsqrt(jnp.mean(o * o, axis=-1, keepdims=True) + EPS) * ng_ref[...]
        o_ref[r0:r0 + CHUNK, :] = (y * _silu(g_ref[r0:r0 + CHUNK, :])).astype(o_ref.dtype)
    s_ref[...] = s


def _deltanet(proj, small, conv_w, alog, dtb, ng, mix, *, b, t, tb):
    m = b * t
    nt = t // tb
    row = lambda bb, hh, tt: bb * nt + tt
    smem = pl.BlockSpec(memory_space=pltpu.SMEM)
    return pl.pallas_call(
        functools.partial(_dn_kernel, tb=tb),
        grid=(b, DN_HEADS, nt),
        in_specs=[
            smem, smem,
            pl.BlockSpec((None, tb, LANE), lambda bb, hh, tt: (CB_DQ + hh, row(bb, hh, tt), 0)),
            pl.BlockSpec((None, tb, LANE), lambda bb, hh, tt: (CB_DK + hh, row(bb, hh, tt), 0)),
            pl.BlockSpec((None, tb, LANE), lambda bb, hh, tt: (CB_DV + hh, row(bb, hh, tt), 0)),
            pl.BlockSpec((None, tb, LANE), lambda bb, hh, tt: (CB_DG + hh, row(bb, hh, tt), 0)),
            pl.BlockSpec((tb, LANE), lambda bb, hh, tt: (row(bb, hh, tt), 0)),
            pl.BlockSpec((CONV_K, LANE), lambda bb, hh, tt: (0, hh)),
            pl.BlockSpec((CONV_K, LANE), lambda bb, hh, tt: (0, DN_HEADS + hh)),
            pl.BlockSpec((CONV_K, LANE), lambda bb, hh, tt: (0, 2 * DN_HEADS + hh)),
            pl.BlockSpec((1, DN_DV), lambda bb, hh, tt: (0, hh)),
            pl.BlockSpec(memory_space=pl.ANY),
        ],
        out_specs=pl.BlockSpec(
            (tb, DN_DV), lambda bb, hh, tt: (row(bb, hh, tt), (RET_HEADS + GLA_HEADS) * 2 + hh)),
        out_shape=jax.ShapeDtypeStruct((m, D_MIX), BF16),
        scratch_shapes=[
            pltpu.VMEM((DN_DK, DN_DV), F32),
            pltpu.VMEM((3, tb + 8, LANE), F32),
        ],
        input_output_aliases={11: 0},
        compiler_params=pltpu.CompilerParams(
            dimension_semantics=("parallel", "parallel", "arbitrary"), vmem_limit_bytes=VMEM_LIMIT),
        name="deltanet",
    )(alog, dtb, proj, proj, proj, proj, small, conv_w, conv_w, conv_w, ng, mix)


def _outproj_kernel(mix_ref, w_ref, x_ref, g_ref, b_ref, o_ref, ob_ref, *, alpha, nk):
    kidx = pl.program_id(1)

    @pl.when(kidx == 0)
    def _():
        o_ref[...] = jnp.zeros_like(o_ref)

    cols = 512
    for n0 in range(0, o_ref.shape[1], cols):
        o_ref[:, n0:n0 + cols] += jnp.dot(mix_ref[...], w_ref[:, n0:n0 + cols], preferred_element_type=F32)

    @pl.when(kidx == nk - 1)
    def _():
        rows = 64
        for r0 in range(0, o_ref.shape[0], rows):
            r = alpha * x_ref[r0:r0 + rows, :] + o_ref[r0:r0 + rows, :]
            mu = jnp.mean(r, axis=-1, keepdims=True)
            d = r - mu
            var = jnp.mean(d * d, axis=-1, keepdims=True)
            y = d * lax.rsqrt(var + LN_EPS) * g_ref[...] + b_ref[...]
            o_ref[r0:r0 + rows, :] = y
            ob_ref[r0:r0 + rows, :] = y.astype(BF16)


def _outproj_ln(mix, wout, x, ln_g, ln_b, *, alpha, tm, tk):
    m, d = x.shape
    nk = D_MIX // tk
    return pl.pallas_call(
        functools.partial(_outproj_kernel, alpha=alpha, nk=nk),
        grid=(m // tm, nk),
        in_specs=[
            pl.BlockSpec((tm, tk), lambda i, kk: (i, kk)),
            pl.BlockSpec((tk, d), lambda i, kk: (kk, 0)),
            pl.BlockSpec((tm, d), lambda i, kk: (i, 0)),
            pl.BlockSpec((1, d), lambda i, kk: (0, 0)),
            pl.BlockSpec((1, d), lambda i, kk: (0, 0)),
        ],
        out_specs=[
            pl.BlockSpec((tm, d), lambda i, kk: (i, 0)),
            pl.BlockSpec((tm, d), lambda i, kk: (i, 0)),
        ],
        out_shape=[jax.ShapeDtypeStruct((m, d), F32), jax.ShapeDtypeStruct((m, d), BF16)],
        compiler_params=pltpu.CompilerParams(
            dimension_semantics=("parallel", "arbitrary"), vmem_limit_bytes=VMEM_LIMIT),
        name="outproj_ln",
    )(mix, wout, x, ln_g, ln_b)


def _pick(n, prefs):
    for p in prefs:
        if n % p == 0:
            return p
    raise ValueError(f"no tile of {prefs} divides {n}")


def _layer(x, xb, pos_col, consts, w_in, gla_w_a2, gla_b_a, dn_conv_w, dn_a_log, dn_dt_bias,
           ret_norm_g, gla_norm_g, dn_norm_g, w_out, ln_g, ln_b, *, b, t, alpha):
    inv2, lg, sel = consts
    m, d = x.shape
    n_a = CB_DQ * LANE
    d0 = n_a + GLA_RANK
    d1 = d0 + (N_CB - CB_DQ) * LANE
    wcat = jnp.concatenate([w_in[:, :n_a], w_in[:, d0:d1]], axis=1).astype(BF16)
    wsmall = jnp.concatenate(
        [w_in[:, n_a:d0], w_in[:, d1:], jnp.zeros((d, LANE - GLA_RANK - 2 * DN_HEADS), w_in.dtype)],
        axis=1).astype(BF16)
    w2pad = jnp.concatenate(
        [gla_w_a2, jnp.zeros((LANE - GLA_RANK, GLA_HEADS * GLA_DK), gla_w_a2.dtype)], axis=0)

    proj, small = _inproj(xb, wcat, wsmall, tm=_pick(m, (1024, 512, 256)), tn=_pick(N_MAIN, (1024,)))
    tb = _pick(t, (256,))
    mix = _retention(proj, pos_col, inv2, lg, ret_norm_g.reshape(1, -1), b=b, t=t, tb=tb)
    mix = _gla(proj, small, w2pad, gla_b_a.reshape(1, -1), gla_norm_g.reshape(1, -1), sel, mix,
               b=b, t=t, tb=tb)
    mix = _deltanet(proj, small, dn_conv_w, dn_a_log, dn_dt_bias, dn_norm_g.reshape(1, -1), mix,
                    b=b, t=t, tb=tb)
    return _outproj_ln(mix, w_out.astype(BF16), x, ln_g.reshape(1, -1), ln_b.reshape(1, -1),
                       alpha=alpha, tm=_pick(m, (256,)), tk=_pick(D_MIX, (512,)))


def kernel(x, positions, w_in, gla_w_a2, gla_b_a, dn_conv_w, dn_a_log, dn_dt_bias, ret_norm_g, gla_norm_g,
           dn_norm_g, w_out, ln_g, ln_b):
    b, t, d = x.shape
    depth = w_in.shape[0]
    alpha = float((2 * depth) ** 0.25)
    half = RET_DK // 2
    inv = ROPE_BASE ** (-jnp.arange(half, dtype=F32) / half)
    inv2 = jnp.concatenate([inv, inv]).reshape(1, LANE)
    lg = jnp.log(1.0 - jnp.power(2.0, -5.0 - jnp.arange(RET_HEADS, dtype=F32)))
    sel = (np.arange(SUB * LANE)[:, None] // LANE == np.arange(CHUNK)[None, :] % SUB)
    consts = (inv2, lg, jnp.asarray(sel, BF16))
    pos_col = positions.reshape(b * t, 1)
    h = x.reshape(b * t, d)
    hb = h.astype(BF16)
    for l in range(depth):
        h, hb = _layer(h, hb, pos_col, consts, w_in[l], gla_w_a2[l], gla_b_a[l], dn_conv_w[l], dn_a_log[l],
                       dn_dt_bias[l], ret_norm_g[l], gla_norm_g[l], dn_norm_g[l], w_out[l], ln_g[l], ln_b[l],
                       b=b, t=t, alpha=alpha)
    return h.reshape(b, t, d)
```

```python
import functools

import jax
import jax.numpy as jnp
import numpy as np
from jax import lax
from jax.experimental import pallas as pl
from jax.experimental.pallas import tpu as pltpu

CHUNK = 64
RET_HEADS, RET_DK, RET_DV = 4, 128, 256
GLA_HEADS, GLA_DK, GLA_DV, GLA_RANK, GLA_TAU = 4, 128, 256, 16, 16.0
DN_HEADS, DN_DK, DN_DV, CONV_K = 16, 128, 128, 4
ROPE_BASE = 10000.0
EPS = 1e-6
LN_EPS = 1e-5

LANE = 128
SUB = 16
DN_GROUP = 4

CB_RQ, CB_RK, CB_RV, CB_RG = 0, 4, 8, 16
CB_GQ, CB_GK, CB_GV, CB_GG = 24, 28, 32, 40
N_CB_A = 48
CB_DQ, CB_DK, CB_DV, CB_DG = 0, 16, 32, 48
N_CB_D = 64
SM_GA, SM_BETA, SM_A = 0, 16, 32
D_MIX = 4096

VMEM_LIMIT = 56 * 1024 * 1024

F32 = jnp.float32
BF16 = jnp.bfloat16
HI = lax.Precision.HIGHEST


def _mm(a, b):
    return jnp.dot(a.astype(BF16), b.astype(BF16), preferred_element_type=F32)


def _mm_nt(a, b):
    return lax.dot_general(a.astype(BF16), b.astype(BF16), (((1,), (1,)), ((), ())),
                           preferred_element_type=F32)


def _mm_tn(a, b):
    return lax.dot_general(a.astype(BF16), b.astype(BF16), (((0,), (0,)), ((), ())),
                           preferred_element_type=F32)


def _silu(x):
    return x * (1.0 / (1.0 + jnp.exp(-x)))


def _softplus(x):
    return jnp.maximum(x, 0.0) + jnp.log1p(jnp.exp(-jnp.abs(x)))


def _chunk_cumsum_matrix(n):
    ii = lax.broadcasted_iota(jnp.int32, (n, n), 0)
    jj = lax.broadcasted_iota(jnp.int32, (n, n), 1)
    return jnp.where((ii >= jj) & ((ii // CHUNK) == (jj // CHUNK)), 1.0, 0.0).astype(F32)


def _inproj_kernel(x_ref, w_ref, ws_ref, p_ref, s_ref, *, tn):
    x = x_ref[...]
    acc = jnp.dot(x, w_ref[...], preferred_element_type=F32)
    for c in range(tn // LANE):
        p_ref[c] = acc[:, c * LANE:(c + 1) * LANE]

    @pl.when(pl.program_id(1) == 0)
    def _():
        s_ref[...] = jnp.dot(x, ws_ref[...], preferred_element_type=F32)


def _inproj(xb, wcat, wsmall, *, tm, tn):
    m, d = xb.shape
    n = wcat.shape[1]
    grid = (m // tm, n // tn)
    return pl.pallas_call(
        functools.partial(_inproj_kernel, tn=tn),
        grid=grid,
        in_specs=[
            pl.BlockSpec((tm, d), lambda i, j: (i, 0)),
            pl.BlockSpec((d, tn), lambda i, j: (0, j)),
            pl.BlockSpec((d, LANE), lambda i, j: (0, 0)),
        ],
        out_specs=[
            pl.BlockSpec((tn // LANE, tm, LANE), lambda i, j: (j, i, 0)),
            pl.BlockSpec((tm, LANE), lambda i, j: (i, 0)),
        ],
        out_shape=[
            jax.ShapeDtypeStruct((n // LANE, m, LANE), F32),
            jax.ShapeDtypeStruct((m, LANE), F32),
        ],
        compiler_params=pltpu.CompilerParams(
            dimension_semantics=("parallel", "arbitrary"), vmem_limit_bytes=VMEM_LIMIT),
        name="inproj",
    )(xb, wcat, wsmall)


def _ret_kernel(lg_ref, q_ref, k_ref, v_ref, g_ref, pos_ref, inv_ref, ng_ref, o_ref, s_ref, *, tb):
    h = pl.program_id(1)

    @pl.when(pl.program_id(2) == 0)
    def _():
        s_ref[...] = jnp.zeros_like(s_ref)

    lg = lg_ref[h]
    ang = pos_ref[...].astype(F32) * inv_ref[...]
    lane = lax.broadcasted_iota(jnp.int32, (1, LANE), 1)
    cosv = jnp.cos(ang)
    sinv = jnp.sin(ang) * jnp.where(lane < LANE // 2, -1.0, 1.0)

    def rot(t):
        return t * cosv + pltpu.roll(t, LANE // 2, 1) * sinv

    q = rot(q_ref[...]) * (RET_DK ** -0.5)
    k = rot(k_ref[...])
    v = jnp.concatenate([v_ref[0], v_ref[1]], axis=1)

    ii = lax.broadcasted_iota(jnp.int32, (tb, tb), 0)
    jj = lax.broadcasted_iota(jnp.int32, (tb, tb), 1)
    causal = ii >= jj
    dmask = jnp.where(causal, jnp.exp(jnp.where(causal, (ii - jj).astype(F32), 0.0) * lg), 0.0)
    ri = lax.broadcasted_iota(jnp.int32, (tb, RET_DV), 0).astype(F32)
    xi = jnp.exp((ri + 1.0) * lg)
    zeta = jnp.exp((tb - 1.0 - ri[:, :RET_DK]) * lg)
    g_chunk = jnp.exp(jnp.full((1, RET_DV), tb * lg, F32))

    s = s_ref[...]
    scores = _mm_nt(q, k) * dmask
    o = _mm(scores, v) + _mm(q, s) * xi
    s_ref[...] = g_chunk * s + _mm_tn(k * zeta, v)

    mu = jnp.mean(o, axis=-1, keepdims=True)
    d = o - mu
    var = jnp.mean(d * d, axis=-1, keepdims=True)
    y = d * lax.rsqrt(var + EPS) * ng_ref[...]
    gate = jnp.concatenate([g_ref[0], g_ref[1]], axis=1)
    o_ref[...] = (y * _silu(gate)).astype(o_ref.dtype)


def _retention(proj, pos_col, inv2, lg, ng, *, b, t, tb):
    m = b * t
    nt = t // tb
    row = lambda bb, hh, tt: bb * nt + tt
    return pl.pallas_call(
        functools.partial(_ret_kernel, tb=tb),
        grid=(b, RET_HEADS, nt),
        in_specs=[
            pl.BlockSpec(memory_space=pltpu.SMEM),
            pl.BlockSpec((None, tb, LANE), lambda bb, hh, tt: (CB_RQ + hh, row(bb, hh, tt), 0)),
            pl.BlockSpec((None, tb, LANE), lambda bb, hh, tt: (CB_RK + hh, row(bb, hh, tt), 0)),
            pl.BlockSpec((2, tb, LANE), lambda bb, hh, tt: (CB_RV // 2 + hh, row(bb, hh, tt), 0)),
            pl.BlockSpec((2, tb, LANE), lambda bb, hh, tt: (CB_RG // 2 + hh, row(bb, hh, tt), 0)),
            pl.BlockSpec((tb, 1), lambda bb, hh, tt: (row(bb, hh, tt), 0)),
            pl.BlockSpec((1, LANE), lambda bb, hh, tt: (0, 0)),
            pl.BlockSpec((1, RET_DV), lambda bb, hh, tt: (0, hh)),
        ],
        out_specs=pl.BlockSpec((tb, RET_DV), lambda bb, hh, tt: (row(bb, hh, tt), hh)),
        out_shape=jax.ShapeDtypeStruct((m, D_MIX), BF16),
        scratch_shapes=[pltpu.VMEM((RET_DK, RET_DV), F32)],
        compiler_params=pltpu.CompilerParams(
            dimension_semantics=("parallel", "parallel", "arbitrary"), vmem_limit_bytes=VMEM_LIMIT),
        name="retention",
    )(lg, proj, proj, proj, proj, pos_col, inv2, ng)


def _gla_kernel(q_ref, k_ref, v_ref, g_ref, sm_ref, w2_ref, ba_ref, ng_ref, sel_ref, mix_in_ref,
                o_ref, st_ref, b_ref, p_ref, *, tb):
    del mix_in_ref

    @pl.when(pl.program_id(2) == 0)
    def _():
        st_ref[...] = jnp.zeros_like(st_ref)

    z = jnp.dot(sm_ref[...], w2_ref[...], precision=HI, preferred_element_type=F32) + ba_ref[...]
    log_a = (jnp.minimum(z, 0.0) - jnp.log1p(jnp.exp(-jnp.abs(z)))) / GLA_TAU
    b_ref[...] = jnp.dot(_chunk_cumsum_matrix(tb), log_a, precision=HI, preferred_element_type=F32)

    rid = lax.broadcasted_iota(jnp.int32, (SUB, LANE), 0)
    ci = lax.broadcasted_iota(jnp.int32, (CHUNK, CHUNK), 0)
    cj = lax.broadcasted_iota(jnp.int32, (CHUNK, CHUNK), 1)
    same_sub = (ci // SUB) == (cj // SUB)
    nsub = CHUNK // SUB
    ng = ng_ref[...]
    sel = sel_ref[...]

    for c in range(tb // CHUNK):
        r0 = c * CHUNK
        qc = q_ref[r0:r0 + CHUNK, :] * (GLA_DK ** -0.5)
        kc = k_ref[r0:r0 + CHUNK, :]
        bc = b_ref[r0:r0 + CHUNK, :]
        vc = jnp.concatenate([v_ref[0, r0:r0 + CHUNK, :], v_ref[1, r0:r0 + CHUNK, :]], axis=1)

        for sb in range(nsub):
            qs = qc[sb * SUB:(sb + 1) * SUB]
            bs = bc[sb * SUB:(sb + 1) * SUB]
            for j in range(SUB):
                r = r0 + sb * SUB + j
                e = jnp.exp(jnp.where(rid >= j, bs - b_ref[r:r + 1, :], -jnp.inf))
                p_ref[sb * SUB:(sb + 1) * SUB, j * LANE:(j + 1) * LANE] = (
                    qs * k_ref[r:r + 1, :] * e).astype(BF16)
        diag = jnp.dot(p_ref[...], sel, preferred_element_type=F32)
        scores = jnp.where(same_sub, diag, 0.0)

        qparts, kparts = [], []
        for sb in range(1, nsub):
            lo = sb * SUB
            bref = bc[lo - 1:lo, :]
            qt = qc[lo:lo + SUB] * jnp.exp(bc[lo:lo + SUB] - bref)
            kt = kc[:lo] * jnp.exp(bref - bc[:lo])
            qparts.append(jnp.concatenate(
                [jnp.zeros((lo, LANE), F32), qt] +
                ([jnp.zeros((CHUNK - lo - SUB, LANE), F32)] if CHUNK - lo - SUB else []), axis=0))
            kparts.append(jnp.concatenate([kt, jnp.zeros((CHUNK - lo, LANE), F32)], axis=0))
        scores = scores + _mm_nt(jnp.concatenate(qparts, axis=1), jnp.concatenate(kparts, axis=1))

        st = st_ref[...]
        o = _mm(scores, vc) + _mm_nt(qc * jnp.exp(bc), st)
        b_last = bc[CHUNK - 1:CHUNK, :]
        st_ref[...] = st * jnp.exp(b_last) + _mm_tn(vc, kc * jnp.exp(b_last - bc))

        y = o * lax.rsqrt(jnp.mean(o * o, axis=-1, keepdims=True) + EPS) * ng
        gate = jnp.concatenate([g_ref[0, r0:r0 + CHUNK, :], g_ref[1, r0:r0 + CHUNK, :]], axis=1)
        o_ref[r0:r0 + CHUNK, :] = (y * _silu(gate)).astype(o_ref.dtype)


def _gla(proj, small, w2pad, ba, ng, sel, mix, *, b, t, tb):
    m = b * t
    nt = t // tb
    row = lambda bb, hh, tt: bb * nt + tt
    return pl.pallas_call(
        functools.partial(_gla_kernel, tb=tb),
        grid=(b, GLA_HEADS, nt),
        in_specs=[
            pl.BlockSpec((None, tb, LANE), lambda bb, hh, tt: (CB_GQ + hh, row(bb, hh, tt), 0)),
            pl.BlockSpec((None, tb, LANE), lambda bb, hh, tt: (CB_GK + hh, row(bb, hh, tt), 0)),
            pl.BlockSpec((2, tb, LANE), lambda bb, hh, tt: (CB_GV // 2 + hh, row(bb, hh, tt), 0)),
            pl.BlockSpec((2, tb, LANE), lambda bb, hh, tt: (CB_GG // 2 + hh, row(bb, hh, tt), 0)),
            pl.BlockSpec((tb, LANE), lambda bb, hh, tt: (row(bb, hh, tt), 0)),
            pl.BlockSpec((LANE, GLA_DK), lambda bb, hh, tt: (0, hh)),
            pl.BlockSpec((1, GLA_DK), lambda bb, hh, tt: (0, hh)),
            pl.BlockSpec((1, GLA_DV), lambda bb, hh, tt: (0, hh)),
            pl.BlockSpec((SUB * LANE, CHUNK), lambda bb, hh, tt: (0, 0)),
            pl.BlockSpec(memory_space=pl.ANY),
        ],
        out_specs=pl.BlockSpec((tb, GLA_DV), lambda bb, hh, tt: (row(bb, hh, tt), RET_HEADS + hh)),
        out_shape=jax.ShapeDtypeStruct((m, D_MIX), BF16),
        scratch_shapes=[
            pltpu.VMEM((GLA_DV, GLA_DK), F32),
            pltpu.VMEM((tb, GLA_DK), F32),
            pltpu.VMEM((CHUNK, SUB * LANE), BF16),
        ],
        input_output_aliases={9: 0},
        compiler_params=pltpu.CompilerParams(
            dimension_semantics=("parallel", "parallel", "arbitrary"), vmem_limit_bytes=VMEM_LIMIT),
        name="gla",
    )(proj, proj, proj, proj, small, w2pad, ba, ng, sel, mix)


def _bf16_parts(x):
    hi = x.astype(BF16)
    r = x - hi.astype(F32)
    mid = r.astype(BF16)
    lo = (r - mid.astype(F32)).astype(BF16)
    return hi, mid, lo


def _bmm(a, b):
    return jnp.einsum("cij,cjk->cik", a.astype(BF16), b.astype(BF16), preferred_element_type=F32)


def _bmm_nt(a, b):
    return jnp.einsum("cik,cjk->cij", a.astype(BF16), b.astype(BF16), preferred_element_type=F32)


def _dn_kernel(q_ref, k_ref, v_ref, g_ref, sm_ref, wq_ref, wk_ref, wv_ref, ng_ref, alog_ref, dtb_ref, pick_ref,
               mix_in_ref, o_ref, s_ref, ext_ref, u_ref, w_ref, qe_ref, ke_ref, qk_ref, egl_ref, oo_ref,
               *, tb, hg):
    del mix_in_ref
    nc = tb // CHUNK

    @pl.when(pl.program_id(2) == 0)
    def _():
        s_ref[...] = jnp.zeros_like(s_ref)
        ext_ref[:, 0:8, :] = jnp.zeros((3 * hg, 8, LANE), F32)

    sm = sm_ref[...]
    lane = lax.broadcasted_iota(jnp.int32, (1, LANE), 1)
    beta_all = 1.0 / (1.0 + jnp.exp(-sm))
    g_all = -jnp.exp(alog_ref[...]) * _softplus(sm + dtb_ref[...])
    gc_all = jnp.dot(_chunk_cumsum_matrix(tb), g_all, precision=HI, preferred_element_type=F32)
    parts = jnp.concatenate(_bf16_parts(jnp.where(lane < SM_A, beta_all, gc_all)), axis=1)
    bg = jnp.dot(parts, pick_ref[...], preferred_element_type=F32)

    ci = lax.broadcasted_iota(jnp.int32, (CHUNK, CHUNK), 0)
    cj = lax.broadcasted_iota(jnp.int32, (CHUNK, CHUNK), 1)
    lower = (ci >= cj)[None]
    strict = (ci > cj)[None]
    eye = jnp.where(ci == cj, 1.0, 0.0).astype(F32)[None]

    def conv(i, raw, w):
        ext_ref[i, 8:8 + tb, :] = raw
        out = ext_ref[i, 8 - (CONV_K - 1):8 - (CONV_K - 1) + tb, :] * w[0:1, :]
        for tap in range(1, CONV_K):
            lo = 8 - (CONV_K - 1) + tap
            out = out + ext_ref[i, lo:lo + tb, :] * w[tap:tap + 1, :]
        ext_ref[i, 0:8, :] = ext_ref[i, tb:tb + 8, :]
        return _silu(out)

    for hh in range(hg):
        cs = slice(hh * LANE, (hh + 1) * LANE)
        q = conv(3 * hh, q_ref[hh], wq_ref[:, cs])
        k = conv(3 * hh + 1, k_ref[hh], wk_ref[:, cs])
        v = conv(3 * hh + 2, v_ref[hh], wv_ref[:, cs])
        q = q * lax.rsqrt(jnp.sum(q * q, axis=-1, keepdims=True) + EPS) * (DN_DK ** -0.5)
        k = k * lax.rsqrt(jnp.sum(k * k, axis=-1, keepdims=True) + EPS)
        beta = bg[:, 2 * hh * LANE:(2 * hh + 1) * LANE]
        gc = bg[:, (2 * hh + 1) * LANE:(2 * hh + 2) * LANE]
        egc = jnp.exp(gc)

        gc3 = gc.reshape(nc, CHUNK, LANE)
        k3 = k.reshape(nc, CHUNK, LANE)
        q3 = q.reshape(nc, CHUNK, LANE)
        diff = gc3[:, :, :CHUNK] - jnp.swapaxes(gc3, 1, 2)[:, :CHUNK, :]
        decay = jnp.exp(jnp.where(lower, diff, -jnp.inf))
        a = jnp.where(strict, beta.reshape(nc, CHUNK, LANE)[:, :, :CHUNK] * _bmm_nt(k3, k3) * decay, 0.0)
        tinv = eye - a
        x = _bmm(a, a)
        for _ in range(4):
            tinv = tinv + _bmm(tinv, x)
            x = _bmm(x, x)
        tinv = tinv + _bmm(tinv, x)
        rhs = jnp.concatenate([v * beta, k * (beta * egc)], axis=1).reshape(nc, CHUNK, 2 * LANE)
        uw = _bmm(tinv, rhs)
        u_ref[hh] = uw[:, :, :DN_DV]
        w_ref[hh] = uw[:, :, DN_DV:].astype(BF16)
        qe_ref[hh] = (q * egc).reshape(nc, CHUNK, LANE).astype(BF16)
        g_last = gc3[:, CHUNK - 1:CHUNK, :]
        ke_ref[hh] = (k3 * jnp.exp(g_last - gc3)).astype(BF16)
        egl_ref[hh] = jnp.exp(g_last)
        qk_ref[hh] = (_bmm_nt(q3, k3) * decay).astype(BF16)

    def chunk_step(c, carry):
        for hh in range(hg):
            s = s_ref[hh]
            sb = s.astype(BF16)
            v_new = u_ref[hh, c] - jnp.dot(w_ref[hh, c], sb, preferred_element_type=F32)
            vb = v_new.astype(BF16)
            oo_ref[hh, c] = (jnp.dot(qe_ref[hh, c], sb, preferred_element_type=F32)
                             + jnp.dot(qk_ref[hh, c], vb, preferred_element_type=F32))
            s_ref[hh] = egl_ref[hh, c] * s + lax.dot_general(
                ke_ref[hh, c], vb, (((0,), (0,)), ((), ())), preferred_element_type=F32)
        return carry

    lax.fori_loop(0, nc, chunk_step, 0)

    for hh in range(hg):
        cs = slice(hh * LANE, (hh + 1) * LANE)
        o = oo_ref[hh].reshape(tb, LANE)
        y = o * lax.rsqrt(jnp.mean(o * o, axis=-1, keepdims=True) + EPS) * ng_ref[:, cs]
        o_ref[:, cs] = (y * _silu(g_ref[hh])).astype(o_ref.dtype)


def _deltanet(proj, small, conv_w, alog_row, dtb_row, pick, ng, mix, *, b, t, tb, hg):
    m = b * t
    nt = t // tb
    ngrp = DN_HEADS // hg
    row = lambda bb, gg, tt: bb * nt + tt
    nc = tb // CHUNK
    return pl.pallas_call(
        functools.partial(_dn_kernel, tb=tb, hg=hg),
        grid=(b, ngrp, nt),
        in_specs=[
            pl.BlockSpec((hg, tb, LANE), lambda bb, gg, tt: (CB_DQ // hg + gg, row(bb, gg, tt), 0)),
            pl.BlockSpec((hg, tb, LANE), lambda bb, gg, tt: (CB_DK // hg + gg, row(bb, gg, tt), 0)),
            pl.BlockSpec((hg, tb, LANE), lambda bb, gg, tt: (CB_DV // hg + gg, row(bb, gg, tt), 0)),
            pl.BlockSpec((hg, tb, LANE), lambda bb, gg, tt: (CB_DG // hg + gg, row(bb, gg, tt), 0)),
            pl.BlockSpec((tb, LANE), lambda bb, gg, tt: (row(bb, gg, tt), 0)),
            pl.BlockSpec((CONV_K, hg * LANE), lambda bb, gg, tt: (0, gg)),
            pl.BlockSpec((CONV_K, hg * LANE), lambda bb, gg, tt: (0, ngrp + gg)),
            pl.BlockSpec((CONV_K, hg * LANE), lambda bb, gg, tt: (0, 2 * ngrp + gg)),
            pl.BlockSpec((1, hg * DN_DV), lambda bb, gg, tt: (0, gg)),
            pl.BlockSpec((1, LANE), lambda bb, gg, tt: (0, 0)),
            pl.BlockSpec((1, LANE), lambda bb, gg, tt: (0, 0)),
            pl.BlockSpec((None, 3 * LANE, hg * 2 * LANE), lambda bb, gg, tt: (gg, 0, 0)),
            pl.BlockSpec(memory_space=pl.ANY),
        ],
        out_specs=pl.BlockSpec(
            (tb, hg * DN_DV),
            lambda bb, gg, tt: (row(bb, gg, tt), (RET_HEADS * RET_DV + GLA_HEADS * GLA_DV) // (hg * DN_DV) + gg)),
        out_shape=jax.ShapeDtypeStruct((m, D_MIX), BF16),
        scratch_shapes=[
            pltpu.VMEM((hg, DN_DK, DN_DV), F32),
            pltpu.VMEM((3 * hg, tb + 8, LANE), F32),
            pltpu.VMEM((hg, nc, CHUNK, DN_DV), F32),
            pltpu.VMEM((hg, nc, CHUNK, DN_DK), BF16),
            pltpu.VMEM((hg, nc, CHUNK, DN_DK), BF16),
            pltpu.VMEM((hg, nc, CHUNK, DN_DK), BF16),
            pltpu.VMEM((hg, nc, CHUNK, CHUNK), BF16),
            pltpu.VMEM((hg, nc, 1, LANE), F32),
            pltpu.VMEM((hg, nc, CHUNK, DN_DV), F32),
        ],
        input_output_aliases={12: 0},
        compiler_params=pltpu.CompilerParams(
            dimension_semantics=("parallel", "parallel", "arbitrary"), vmem_limit_bytes=VMEM_LIMIT),
        name="deltanet",
    )(proj, proj, proj, proj, small, conv_w, conv_w, conv_w, ng, alog_row, dtb_row, pick, mix)


def _dn_pick_table(hg):
    ngrp = DN_HEADS // hg
    src = np.arange(3 * LANE) % LANE
    col = np.arange(hg * 2 * LANE)
    hh, is_g = col // (2 * LANE), (col // LANE) % 2
    table = np.zeros((ngrp, 3 * LANE, hg * 2 * LANE), np.float32)
    for grp in range(ngrp):
        want = np.where(is_g == 1, SM_A, SM_BETA) + grp * hg + hh
        table[grp] = src[:, None] == want[None, :]
    return table


def _outproj_kernel(mix_ref, w_ref, x_ref, g_ref, b_ref, o_ref, ob_ref, *, alpha, nk):
    kidx = pl.program_id(1)

    @pl.when(kidx == 0)
    def _():
        o_ref[...] = jnp.zeros_like(o_ref)

    cols = 512
    for n0 in range(0, o_ref.shape[1], cols):
        o_ref[:, n0:n0 + cols] += jnp.dot(mix_ref[...], w_ref[:, n0:n0 + cols], preferred_element_type=F32)

    @pl.when(kidx == nk - 1)
    def _():
        rows = 64
        for r0 in range(0, o_ref.shape[0], rows):
            r = alpha * x_ref[r0:r0 + rows, :] + o_ref[r0:r0 + rows, :]
            mu = jnp.mean(r, axis=-1, keepdims=True)
            d = r - mu
            var = jnp.mean(d * d, axis=-1, keepdims=True)
            y = d * lax.rsqrt(var + LN_EPS) * g_ref[...] + b_ref[...]
            o_ref[r0:r0 + rows, :] = y
            ob_ref[r0:r0 + rows, :] = y.astype(BF16)


def _outproj_ln(mix, wout, x, ln_g, ln_b, *, alpha, tm, tk):
    m, d = x.shape
    nk = D_MIX // tk
    return pl.pallas_call(
        functools.partial(_outproj_kernel, alpha=alpha, nk=nk),
        grid=(m // tm, nk),
        in_specs=[
            pl.BlockSpec((tm, tk), lambda i, kk: (i, kk)),
            pl.BlockSpec((tk, d), lambda i, kk: (kk, 0)),
            pl.BlockSpec((tm, d), lambda i, kk: (i, 0)),
            pl.BlockSpec((1, d), lambda i, kk: (0, 0)),
            pl.BlockSpec((1, d), lambda i, kk: (0, 0)),
        ],
        out_specs=[
            pl.BlockSpec((tm, d), lambda i, kk: (i, 0)),
            pl.BlockSpec((tm, d), lambda i, kk: (i, 0)),
        ],
        out_shape=[jax.ShapeDtypeStruct((m, d), F32), jax.ShapeDtypeStruct((m, d), BF16)],
        compiler_params=pltpu.CompilerParams(
            dimension_semantics=("parallel", "arbitrary"), vmem_limit_bytes=VMEM_LIMIT),
        name="outproj_ln",
    )(mix, wout, x, ln_g, ln_b)


def _pick(n, prefs):
    for p in prefs:
        if n % p == 0:
            return p
    raise ValueError(f"no tile of {prefs} divides {n}")


def _layer(x, xb, pos_col, consts, w_in, gla_w_a2, gla_b_a, dn_conv_w, dn_a_log, dn_dt_bias,
           ret_norm_g, gla_norm_g, dn_norm_g, w_out, ln_g, ln_b, *, b, t, alpha):
    inv2, lg, sel, pick = consts
    m, d = x.shape
    n_a = N_CB_A * LANE
    d0 = n_a + GLA_RANK
    d1 = d0 + N_CB_D * LANE
    zcols = lambda n: jnp.zeros((d, n), BF16)
    w_a = w_in[:, :n_a].astype(BF16)
    w_d = w_in[:, d0:d1].astype(BF16)
    ws_a = jnp.concatenate([w_in[:, n_a:d0].astype(BF16), zcols(LANE - GLA_RANK)], axis=1)
    ws_d = jnp.concatenate(
        [zcols(SM_BETA), w_in[:, d1:].astype(BF16), zcols(LANE - SM_BETA - 2 * DN_HEADS)], axis=1)
    w2pad = jnp.concatenate(
        [gla_w_a2, jnp.zeros((LANE - GLA_RANK, GLA_HEADS * GLA_DK), gla_w_a2.dtype)], axis=0)

    tm = _pick(m, (1024, 512, 256))
    proj_a, small_a = _inproj(xb, w_a, ws_a, tm=tm, tn=_pick(n_a, (1024,)))
    proj_d, small_d = _inproj(xb, w_d, ws_d, tm=tm, tn=_pick(d1 - d0, (1024,)))
    tb = _pick(t, (256,))
    mix = _retention(proj_a, pos_col, inv2, lg, ret_norm_g.reshape(1, -1), b=b, t=t, tb=tb)
    mix = _gla(proj_a, small_a, w2pad, gla_b_a.reshape(1, -1), gla_norm_g.reshape(1, -1), sel, mix,
               b=b, t=t, tb=tb)
    lane_pad = lambda p: jnp.zeros((1, LANE), F32).at[0, SM_A:SM_A + DN_HEADS].set(p)
    mix = _deltanet(proj_d, small_d, dn_conv_w, lane_pad(dn_a_log), lane_pad(dn_dt_bias), pick,
                    dn_norm_g.reshape(1, -1), mix, b=b, t=t, tb=_pick(t, (512, 256)), hg=DN_GROUP)
    return _outproj_ln(mix, w_out.astype(BF16), x, ln_g.reshape(1, -1), ln_b.reshape(1, -1),
                       alpha=alpha, tm=_pick(m, (512, 256)), tk=_pick(D_MIX, (256,)))


def kernel(x, positions, w_in, gla_w_a2, gla_b_a, dn_conv_w, dn_a_log, dn_dt_bias, ret_norm_g, gla_norm_g,
           dn_norm_g, w_out, ln_g, ln_b):
    b, t, d = x.shape
    depth = w_in.shape[0]
    alpha = float((2 * depth) ** 0.25)
    half = RET_DK // 2
    inv = ROPE_BASE ** (-jnp.arange(half, dtype=F32) / half)
    inv2 = jnp.concatenate([inv, inv]).reshape(1, LANE)
    lg = jnp.log(1.0 - jnp.power(2.0, -5.0 - jnp.arange(RET_HEADS, dtype=F32)))
    sel = (np.arange(SUB * LANE)[:, None] // LANE == np.arange(CHUNK)[None, :] % SUB)
    consts = (inv2, lg, jnp.asarray(sel, BF16), jnp.asarray(_dn_pick_table(DN_GROUP), BF16))
    pos_col = positions.reshape(b * t, 1)
    h = x.reshape(b * t, d)
    hb = h.astype(BF16)
    for l in range(depth):
        h, hb = _layer(h, hb, pos_col, consts, w_in[l], gla_w_a2[l], gla_b_a[l], dn_conv_w[l], dn_a_log[l],
                       dn_dt_bias[l], ret_norm_g[l], gla_norm_g[l], dn_norm_g[l], w_out[l], ln_g[l], ln_b[l],
                       b=b, t=t, alpha=alpha)
    return h.reshape(b, t, d)
```

```python
import functools

import jax
import jax.numpy as jnp
import numpy as np
from jax import lax
from jax.experimental import pallas as pl
from jax.experimental.pallas import tpu as pltpu

CHUNK = 64
RET_HEADS, RET_DK, RET_DV = 4, 128, 256
GLA_HEADS, GLA_DK, GLA_DV, GLA_RANK, GLA_TAU = 4, 128, 256, 16, 16.0
DN_HEADS, DN_DK, DN_DV, CONV_K = 16, 128, 128, 4
ROPE_BASE = 10000.0
EPS = 1e-6
LN_EPS = 1e-5

LANE = 128
SUB = 16
DN_GROUP = 4

CB_RQ, CB_RK, CB_RV, CB_RG = 0, 4, 8, 16
CB_GQ, CB_GK, CB_GV, CB_GG = 24, 28, 32, 40
CB_DQ, CB_DK, CB_DV, CB_DG = 48, 64, 80, 96
N_CB_A = 48
N_CB = 112
SM_GA, SM_BETA, SM_A = 0, 16, 32
D_MIX = 4096

VMEM_LIMIT = 56 * 1024 * 1024

F32 = jnp.float32
BF16 = jnp.bfloat16
HI = lax.Precision.HIGHEST


def _mm(a, b):
    return jnp.dot(a.astype(BF16), b.astype(BF16), preferred_element_type=F32)


def _mm_nt(a, b):
    return lax.dot_general(a.astype(BF16), b.astype(BF16), (((1,), (1,)), ((), ())),
                           preferred_element_type=F32)


def _mm_tn(a, b):
    return lax.dot_general(a.astype(BF16), b.astype(BF16), (((0,), (0,)), ((), ())),
                           preferred_element_type=F32)


def _silu(x):
    return x * (1.0 / (1.0 + jnp.exp(-x)))


def _softplus(x):
    return jnp.maximum(x, 0.0) + jnp.log1p(jnp.exp(-jnp.abs(x)))


def _chunk_cumsum_matrix(n):
    ii = lax.broadcasted_iota(jnp.int32, (n, n), 0)
    jj = lax.broadcasted_iota(jnp.int32, (n, n), 1)
    return jnp.where((ii >= jj) & ((ii // CHUNK) == (jj // CHUNK)), 1.0, 0.0).astype(F32)


def _inproj_kernel(x_ref, w_ref, ws_ref, p_ref, s_ref, *, tn):
    x = x_ref[...]
    acc = jnp.dot(x, w_ref[...], preferred_element_type=F32)
    for c in range(tn // LANE):
        p_ref[c] = acc[:, c * LANE:(c + 1) * LANE]

    @pl.when(pl.program_id(1) == 0)
    def _():
        s_ref[...] = jnp.dot(x, ws_ref[...], preferred_element_type=F32)


def _wprep_kernel(a_ref, nx_ref, ga_ref, bd_ref, o_ref, s_ref, *, tn, na):
    j = pl.program_id(2)
    lane = lax.broadcasted_iota(jnp.int32, (1, LANE), 1)

    @pl.when(j < na)
    def _():
        o_ref[...] = a_ref[...].astype(BF16)

    @pl.when(j >= na)
    def _():
        keep = LANE - GLA_RANK
        nblk = tn // LANE
        rolled = [pltpu.roll(a_ref[:, c * LANE:(c + 1) * LANE], keep, 1) for c in range(nblk)]
        rolled.append(pltpu.roll(nx_ref[...], keep, 1))
        for c in range(nblk):
            o_ref[:, c * LANE:(c + 1) * LANE] = jnp.where(lane < keep, rolled[c], rolled[c + 1]).astype(BF16)

    @pl.when(j == 0)
    def _():
        s_ref[...] = jnp.where(lane < GLA_RANK, ga_ref[...],
                               jnp.where(lane < SM_A + DN_HEADS, bd_ref[...], 0.0)).astype(BF16)


def _wprep(w_in, *, tr, tn):
    depth, d, _ = w_in.shape
    na = N_CB_A * LANE // tn
    per = tn // LANE
    return pl.pallas_call(
        functools.partial(_wprep_kernel, tn=tn, na=na),
        grid=(depth, d // tr, N_CB * LANE // tn),
        in_specs=[
            pl.BlockSpec((None, tr, tn), lambda l, r, j: (l, r, j)),
            pl.BlockSpec((None, tr, LANE), lambda l, r, j: (l, r, (j + 1) * per)),
            pl.BlockSpec((None, tr, LANE), lambda l, r, j: (l, r, N_CB_A)),
            pl.BlockSpec((None, tr, LANE), lambda l, r, j: (l, r, N_CB)),
        ],
        out_specs=[
            pl.BlockSpec((None, tr, tn), lambda l, r, j: (l, r, j)),
            pl.BlockSpec((None, tr, LANE), lambda l, r, j: (l, r, 0)),
        ],
        out_shape=[
            jax.ShapeDtypeStruct((depth, d, N_CB * LANE), BF16),
            jax.ShapeDtypeStruct((depth, d, LANE), BF16),
        ],
        compiler_params=pltpu.CompilerParams(
            dimension_semantics=("parallel", "parallel", "arbitrary"), vmem_limit_bytes=VMEM_LIMIT),
        name="wprep",
    )(w_in, w_in, w_in, w_in)


def _inproj(xb, wcat, wsmall, layer, *, tm, tn):
    m, d = xb.shape
    n = wcat.shape[2]
    grid = (m // tm, n // tn)
    return pl.pallas_call(
        functools.partial(_inproj_kernel, tn=tn),
        grid=grid,
        in_specs=[
            pl.BlockSpec((tm, d), lambda i, j: (i, 0)),
            pl.BlockSpec((None, d, tn), lambda i, j: (layer, 0, j)),
            pl.BlockSpec((None, d, LANE), lambda i, j: (layer, 0, 0)),
        ],
        out_specs=[
            pl.BlockSpec((tn // LANE, tm, LANE), lambda i, j: (j, i, 0)),
            pl.BlockSpec((tm, LANE), lambda i, j: (i, 0)),
        ],
        out_shape=[
            jax.ShapeDtypeStruct((n // LANE, m, LANE), F32),
            jax.ShapeDtypeStruct((m, LANE), F32),
        ],
        compiler_params=pltpu.CompilerParams(
            dimension_semantics=("parallel", "arbitrary"), vmem_limit_bytes=VMEM_LIMIT),
        name="inproj",
    )(xb, wcat, wsmall)


def _ret_kernel(lg_ref, q_ref, k_ref, v_ref, g_ref, pos_ref, inv_ref, ng_ref, o_ref, s_ref, *, tb):
    h = pl.program_id(1)

    @pl.when(pl.program_id(2) == 0)
    def _():
        s_ref[...] = jnp.zeros_like(s_ref)

    lg = lg_ref[h]
    ang = pos_ref[...].astype(F32) * inv_ref[...]
    lane = lax.broadcasted_iota(jnp.int32, (1, LANE), 1)
    cosv = jnp.cos(ang)
    sinv = jnp.sin(ang) * jnp.where(lane < LANE // 2, -1.0, 1.0)

    def rot(t):
        return t * cosv + pltpu.roll(t, LANE // 2, 1) * sinv

    q = rot(q_ref[...]) * (RET_DK ** -0.5)
    k = rot(k_ref[...])
    v = jnp.concatenate([v_ref[0], v_ref[1]], axis=1)

    ii = lax.broadcasted_iota(jnp.int32, (tb, tb), 0)
    jj = lax.broadcasted_iota(jnp.int32, (tb, tb), 1)
    causal = ii >= jj
    dmask = jnp.where(causal, jnp.exp(jnp.where(causal, (ii - jj).astype(F32), 0.0) * lg), 0.0)
    ri = lax.broadcasted_iota(jnp.int32, (tb, RET_DV), 0).astype(F32)
    xi = jnp.exp((ri + 1.0) * lg)
    zeta = jnp.exp((tb - 1.0 - ri[:, :RET_DK]) * lg)
    g_chunk = jnp.exp(jnp.full((1, RET_DV), tb * lg, F32))

    s = s_ref[...]
    scores = _mm_nt(q, k) * dmask
    o = _mm(scores, v) + _mm(q, s) * xi
    s_ref[...] = g_chunk * s + _mm_tn(k * zeta, v)

    mu = jnp.mean(o, axis=-1, keepdims=True)
    d = o - mu
    var = jnp.mean(d * d, axis=-1, keepdims=True)
    y = d * lax.rsqrt(var + EPS) * ng_ref[...]
    gate = jnp.concatenate([g_ref[0], g_ref[1]], axis=1)
    o_ref[...] = (y * _silu(gate)).astype(o_ref.dtype)


def _retention(proj, pos_col, inv2, lg, ng, *, b, t, tb):
    m = b * t
    nt = t // tb
    row = lambda bb, hh, tt: bb * nt + tt
    return pl.pallas_call(
        functools.partial(_ret_kernel, tb=tb),
        grid=(b, RET_HEADS, nt),
        in_specs=[
            pl.BlockSpec(memory_space=pltpu.SMEM),
            pl.BlockSpec((None, tb, LANE), lambda bb, hh, tt: (CB_RQ + hh, row(bb, hh, tt), 0)),
            pl.BlockSpec((None, tb, LANE), lambda bb, hh, tt: (CB_RK + hh, row(bb, hh, tt), 0)),
            pl.BlockSpec((2, tb, LANE), lambda bb, hh, tt: (CB_RV // 2 + hh, row(bb, hh, tt), 0)),
            pl.BlockSpec((2, tb, LANE), lambda bb, hh, tt: (CB_RG // 2 + hh, row(bb, hh, tt), 0)),
            pl.BlockSpec((tb, 1), lambda bb, hh, tt: (row(bb, hh, tt), 0)),
            pl.BlockSpec((1, LANE), lambda bb, hh, tt: (0, 0)),
            pl.BlockSpec((1, RET_DV), lambda bb, hh, tt: (0, hh)),
        ],
        out_specs=pl.BlockSpec((tb, RET_DV), lambda bb, hh, tt: (row(bb, hh, tt), hh)),
        out_shape=jax.ShapeDtypeStruct((m, D_MIX), BF16),
        scratch_shapes=[pltpu.VMEM((RET_DK, RET_DV), F32)],
        compiler_params=pltpu.CompilerParams(
            dimension_semantics=("parallel", "parallel", "arbitrary"), vmem_limit_bytes=VMEM_LIMIT),
        name="retention",
    )(lg, proj, proj, proj, proj, pos_col, inv2, ng)


def _gla_kernel(q_ref, k_ref, v_ref, g_ref, sm_ref, w2_ref, ba_ref, ng_ref, sel_ref, mix_in_ref,
                o_ref, st_ref, b_ref, p_ref, *, tb):
    del mix_in_ref

    @pl.when(pl.program_id(2) == 0)
    def _():
        st_ref[...] = jnp.zeros_like(st_ref)

    z = jnp.dot(sm_ref[...], w2_ref[...], precision=HI, preferred_element_type=F32) + ba_ref[...]
    log_a = (jnp.minimum(z, 0.0) - jnp.log1p(jnp.exp(-jnp.abs(z)))) / GLA_TAU
    b_ref[...] = jnp.dot(_chunk_cumsum_matrix(tb), log_a, precision=HI, preferred_element_type=F32)

    rid = lax.broadcasted_iota(jnp.int32, (SUB, LANE), 0)
    ci = lax.broadcasted_iota(jnp.int32, (CHUNK, CHUNK), 0)
    cj = lax.broadcasted_iota(jnp.int32, (CHUNK, CHUNK), 1)
    same_sub = (ci // SUB) == (cj // SUB)
    nsub = CHUNK // SUB
    ng = ng_ref[...]
    sel = sel_ref[...]

    for c in range(tb // CHUNK):
        r0 = c * CHUNK
        qc = q_ref[r0:r0 + CHUNK, :] * (GLA_DK ** -0.5)
        kc = k_ref[r0:r0 + CHUNK, :]
        bc = b_ref[r0:r0 + CHUNK, :]
        vc = jnp.concatenate([v_ref[0, r0:r0 + CHUNK, :], v_ref[1, r0:r0 + CHUNK, :]], axis=1)

        for sb in range(nsub):
            qs = qc[sb * SUB:(sb + 1) * SUB]
            bs = bc[sb * SUB:(sb + 1) * SUB]
            for j in range(SUB):
                r = r0 + sb * SUB + j
                e = jnp.exp(jnp.where(rid >= j, bs - b_ref[r:r + 1, :], -jnp.inf))
                p_ref[sb * SUB:(sb + 1) * SUB, j * LANE:(j + 1) * LANE] = (
                    qs * k_ref[r:r + 1, :] * e).astype(BF16)
        diag = jnp.dot(p_ref[...], sel, preferred_element_type=F32)
        scores = jnp.where(same_sub, diag, 0.0)

        qparts, kparts = [], []
        for sb in range(1, nsub):
            lo = sb * SUB
            bref = bc[lo - 1:lo, :]
            qt = qc[lo:lo + SUB] * jnp.exp(bc[lo:lo + SUB] - bref)
            kt = kc[:lo] * jnp.exp(bref - bc[:lo])
            qparts.append(jnp.concatenate(
                [jnp.zeros((lo, LANE), F32), qt] +
                ([jnp.zeros((CHUNK - lo - SUB, LANE), F32)] if CHUNK - lo - SUB else []), axis=0))
            kparts.append(jnp.concatenate([kt, jnp.zeros((CHUNK - lo, LANE), F32)], axis=0))
        scores = scores + _mm_nt(jnp.concatenate(qparts, axis=1), jnp.concatenate(kparts, axis=1))

        st = st_ref[...]
        o = _mm(scores, vc) + _mm_nt(qc * jnp.exp(bc), st)
        b_last = bc[CHUNK - 1:CHUNK, :]
        st_ref[...] = st * jnp.exp(b_last) + _mm_tn(vc, kc * jnp.exp(b_last - bc))

        y = o * lax.rsqrt(jnp.mean(o * o, axis=-1, keepdims=True) + EPS) * ng
        gate = jnp.concatenate([g_ref[0, r0:r0 + CHUNK, :], g_ref[1, r0:r0 + CHUNK, :]], axis=1)
        o_ref[r0:r0 + CHUNK, :] = (y * _silu(gate)).astype(o_ref.dtype)


def _gla(proj, small, w2pad, ba, ng, sel, mix, *, b, t, tb):
    m = b * t
    nt = t // tb
    row = lambda bb, hh, tt: bb * nt + tt
    return pl.pallas_call(
        functools.partial(_gla_kernel, tb=tb),
        grid=(b, GLA_HEADS, nt),
        in_specs=[
            pl.BlockSpec((None, tb, LANE), lambda bb, hh, tt: (CB_GQ + hh, row(bb, hh, tt), 0)),
            pl.BlockSpec((None, tb, LANE), lambda bb, hh, tt: (CB_GK + hh, row(bb, hh, tt), 0)),
            pl.BlockSpec((2, tb, LANE), lambda bb, hh, tt: (CB_GV // 2 + hh, row(bb, hh, tt), 0)),
            pl.BlockSpec((2, tb, LANE), lambda bb, hh, tt: (CB_GG // 2 + hh, row(bb, hh, tt), 0)),
            pl.BlockSpec((tb, LANE), lambda bb, hh, tt: (row(bb, hh, tt), 0)),
            pl.BlockSpec((LANE, GLA_DK), lambda bb, hh, tt: (0, hh)),
            pl.BlockSpec((1, GLA_DK), lambda bb, hh, tt: (0, hh)),
            pl.BlockSpec((1, GLA_DV), lambda bb, hh, tt: (0, hh)),
            pl.BlockSpec((SUB * LANE, CHUNK), lambda bb, hh, tt: (0, 0)),
            pl.BlockSpec(memory_space=pl.ANY),
        ],
        out_specs=pl.BlockSpec((tb, GLA_DV), lambda bb, hh, tt: (row(bb, hh, tt), RET_HEADS + hh)),
        out_shape=jax.ShapeDtypeStruct((m, D_MIX), BF16),
        scratch_shapes=[
            pltpu.VMEM((GLA_DV, GLA_DK), F32),
            pltpu.VMEM((tb, GLA_DK), F32),
            pltpu.VMEM((CHUNK, SUB * LANE), BF16),
        ],
        input_output_aliases={9: 0},
        compiler_params=pltpu.CompilerParams(
            dimension_semantics=("parallel", "parallel", "arbitrary"), vmem_limit_bytes=VMEM_LIMIT),
        name="gla",
    )(proj, proj, proj, proj, small, w2pad, ba, ng, sel, mix)


def _bf16_parts(x):
    hi = x.astype(BF16)
    r = x - hi.astype(F32)
    mid = r.astype(BF16)
    lo = (r - mid.astype(F32)).astype(BF16)
    return hi, mid, lo


def _bmm(a, b):
    return jnp.einsum("cij,cjk->cik", a.astype(BF16), b.astype(BF16), preferred_element_type=F32)


def _bmm_nt(a, b):
    return jnp.einsum("cik,cjk->cij", a.astype(BF16), b.astype(BF16), preferred_element_type=F32)


def _dn_kernel(q_ref, k_ref, v_ref, g_ref, sm_ref, wq_ref, wk_ref, wv_ref, ng_ref, alog_ref, dtb_ref, pick_ref,
               mix_in_ref, o_ref, s_ref, ext_ref, wp_ref, np_ref, sb_ref, qp_ref, qu_ref, egl_ref,
               *, tb, hg):
    del mix_in_ref
    nc = tb // CHUNK

    @pl.when(pl.program_id(2) == 0)
    def _():
        s_ref[...] = jnp.zeros_like(s_ref)
        ext_ref[:, 0:8, :] = jnp.zeros((3 * hg, 8, LANE), F32)

    sm = sm_ref[...]
    lane = lax.broadcasted_iota(jnp.int32, (1, LANE), 1)
    beta_all = 1.0 / (1.0 + jnp.exp(-sm))
    g_all = -jnp.exp(alog_ref[...]) * _softplus(sm + dtb_ref[...])
    ci = lax.broadcasted_iota(jnp.int32, (CHUNK, CHUNK), 0)
    cj = lax.broadcasted_iota(jnp.int32, (CHUNK, CHUNK), 1)
    tri = jnp.broadcast_to(jnp.where(ci >= cj, 1.0, 0.0).astype(BF16)[None], (nc, CHUNK, CHUNK))
    gc_all = sum(jnp.einsum("cij,cjk->cik", tri, p.reshape(nc, CHUNK, LANE), preferred_element_type=F32)
                 for p in _bf16_parts(g_all)).reshape(tb, LANE)
    parts = jnp.concatenate(_bf16_parts(jnp.where(lane < SM_A, beta_all, gc_all)), axis=1)
    bg = jnp.dot(parts, pick_ref[...], preferred_element_type=F32)

    lower = (ci >= cj)[None]
    strict = (ci > cj)[None]
    eye = jnp.where(ci == cj, 1.0, 0.0).astype(F32)[None]

    def conv(i, raw, w):
        ext_ref[i, 8:8 + tb, :] = raw
        out = ext_ref[i, 8 - (CONV_K - 1):8 - (CONV_K - 1) + tb, :] * w[0:1, :]
        for tap in range(1, CONV_K):
            lo = 8 - (CONV_K - 1) + tap
            out = out + ext_ref[i, lo:lo + tb, :] * w[tap:tap + 1, :]
        ext_ref[i, 0:8, :] = ext_ref[i, tb:tb + 8, :]
        return _silu(out)

    for hh in range(hg):
        cs = slice(hh * LANE, (hh + 1) * LANE)
        q = conv(3 * hh, q_ref[hh], wq_ref[:, cs])
        k = conv(3 * hh + 1, k_ref[hh], wk_ref[:, cs])
        v = conv(3 * hh + 2, v_ref[hh], wv_ref[:, cs])
        q = q * lax.rsqrt(jnp.sum(q * q, axis=-1, keepdims=True) + EPS) * (DN_DK ** -0.5)
        k = k * lax.rsqrt(jnp.sum(k * k, axis=-1, keepdims=True) + EPS)
        beta = bg[:, 2 * hh * LANE:(2 * hh + 1) * LANE]
        gc = bg[:, (2 * hh + 1) * LANE:(2 * hh + 2) * LANE]
        egc = jnp.exp(gc)

        gc3 = gc.reshape(nc, CHUNK, LANE)
        k3 = k.reshape(nc, CHUNK, LANE)
        q3 = q.reshape(nc, CHUNK, LANE)
        diff = gc3[:, :, :CHUNK] - jnp.swapaxes(gc3, 1, 2)[:, :CHUNK, :]
        decay = jnp.exp(jnp.where(lower, diff, -jnp.inf))
        a = jnp.where(strict, beta.reshape(nc, CHUNK, LANE)[:, :, :CHUNK] * _bmm_nt(k3, k3) * decay, 0.0)
        tinv = eye - a
        x = _bmm(a, a)
        for _ in range(4):
            tinv = tinv + _bmm(tinv, x)
            x = _bmm(x, x)
        tinv = tinv + _bmm(tinv, x)
        rhs = jnp.concatenate([v * beta, k * (beta * egc)], axis=1).reshape(nc, CHUNK, 2 * LANE)
        wu = _bmm(tinv, rhs).astype(BF16)
        g_last = gc3[:, CHUNK - 1:CHUNK, :]
        ke = (k3 * jnp.exp(g_last - gc3)).astype(BF16)
        qk = (_bmm_nt(q3, k3) * decay).astype(BF16)
        kt = jnp.einsum("cjk,cjd->ckd", ke, wu, preferred_element_type=F32)
        np_ref[hh] = kt[:, :, :DN_DV]
        wp_ref[hh] = kt[:, :, DN_DV:].astype(BF16)
        qt = jnp.einsum("cij,cjd->cid", qk, wu, preferred_element_type=F32)
        qu_ref[hh] = qt[:, :, :DN_DV]
        qp_ref[hh] = ((q * egc).reshape(nc, CHUNK, LANE) - qt[:, :, DN_DV:]).astype(BF16)
        egl_ref[hh] = jnp.exp(g_last)

    def chunk_step(c, carry):
        for hh in range(hg):
            s = s_ref[hh]
            sb = s.astype(BF16)
            sb_ref[hh, c] = sb
            s_ref[hh] = egl_ref[hh, c] * s + np_ref[hh, c] - jnp.dot(wp_ref[hh, c], sb, preferred_element_type=F32)
        return carry

    lax.fori_loop(0, nc, chunk_step, 0)

    for hh in range(hg):
        cs = slice(hh * LANE, (hh + 1) * LANE)
        o = (jnp.einsum("cik,ckd->cid", qp_ref[hh], sb_ref[hh], preferred_element_type=F32)
             + qu_ref[hh]).reshape(tb, LANE)
        y = o * lax.rsqrt(jnp.mean(o * o, axis=-1, keepdims=True) + EPS) * ng_ref[:, cs]
        o_ref[:, cs] = (y * _silu(g_ref[hh])).astype(o_ref.dtype)


def _deltanet(proj, small, conv_w, alog_row, dtb_row, pick, ng, mix, *, b, t, tb, hg):
    m = b * t
    nt = t // tb
    ngrp = DN_HEADS // hg
    row = lambda bb, gg, tt: bb * nt + tt
    nc = tb // CHUNK
    return pl.pallas_call(
        functools.partial(_dn_kernel, tb=tb, hg=hg),
        grid=(b, ngrp, nt),
        in_specs=[
            pl.BlockSpec((hg, tb, LANE), lambda bb, gg, tt: (CB_DQ // hg + gg, row(bb, gg, tt), 0)),
            pl.BlockSpec((hg, tb, LANE), lambda bb, gg, tt: (CB_DK // hg + gg, row(bb, gg, tt), 0)),
            pl.BlockSpec((hg, tb, LANE), lambda bb, gg, tt: (CB_DV // hg + gg, row(bb, gg, tt), 0)),
            pl.BlockSpec((hg, tb, LANE), lambda bb, gg, tt: (CB_DG // hg + gg, row(bb, gg, tt), 0)),
            pl.BlockSpec((tb, LANE), lambda bb, gg, tt: (row(bb, gg, tt), 0)),
            pl.BlockSpec((CONV_K, hg * LANE), lambda bb, gg, tt: (0, gg)),
            pl.BlockSpec((CONV_K, hg * LANE), lambda bb, gg, tt: (0, ngrp + gg)),
            pl.BlockSpec((CONV_K, hg * LANE), lambda bb, gg, tt: (0, 2 * ngrp + gg)),
            pl.BlockSpec((1, hg * DN_DV), lambda bb, gg, tt: (0, gg)),
            pl.BlockSpec((1, LANE), lambda bb, gg, tt: (0, 0)),
            pl.BlockSpec((1, LANE), lambda bb, gg, tt: (0, 0)),
            pl.BlockSpec((None, 3 * LANE, hg * 2 * LANE), lambda bb, gg, tt: (gg, 0, 0)),
            pl.BlockSpec(memory_space=pl.ANY),
        ],
        out_specs=pl.BlockSpec(
            (tb, hg * DN_DV),
            lambda bb, gg, tt: (row(bb, gg, tt), (RET_HEADS * RET_DV + GLA_HEADS * GLA_DV) // (hg * DN_DV) + gg)),
        out_shape=jax.ShapeDtypeStruct((m, D_MIX), BF16),
        scratch_shapes=[
            pltpu.VMEM((hg, DN_DK, DN_DV), F32),
            pltpu.VMEM((3 * hg, tb + 8, LANE), F32),
            pltpu.VMEM((hg, nc, DN_DK, DN_DK), BF16),
            pltpu.VMEM((hg, nc, DN_DK, DN_DV), F32),
            pltpu.VMEM((hg, nc, DN_DK, DN_DV), BF16),
            pltpu.VMEM((hg, nc, CHUNK, DN_DK), BF16),
            pltpu.VMEM((hg, nc, CHUNK, DN_DV), F32),
            pltpu.VMEM((hg, nc, 1, LANE), F32),
        ],
        input_output_aliases={12: 0},
        compiler_params=pltpu.CompilerParams(
            dimension_semantics=("parallel", "parallel", "arbitrary"), vmem_limit_bytes=VMEM_LIMIT),
        name="deltanet",
    )(proj, proj, proj, proj, small, conv_w, conv_w, conv_w, ng, alog_row, dtb_row, pick, mix)


def _dn_pick_table(hg):
    ngrp = DN_HEADS // hg
    src = np.arange(3 * LANE) % LANE
    col = np.arange(hg * 2 * LANE)
    hh, is_g = col // (2 * LANE), (col // LANE) % 2
    table = np.zeros((ngrp, 3 * LANE, hg * 2 * LANE), np.float32)
    for grp in range(ngrp):
        want = np.where(is_g == 1, SM_A, SM_BETA) + grp * hg + hh
        table[grp] = src[:, None] == want[None, :]
    return table


def _outproj_kernel(mix_ref, w_ref, x_ref, g_ref, b_ref, o_ref, ob_ref, *, alpha, nk):
    kidx = pl.program_id(1)

    cols = 512

    def partial_product(n0):
        return jnp.dot(mix_ref[...], w_ref[:, n0:n0 + cols], preferred_element_type=F32)

    @pl.when(kidx == 0)
    def _():
        for n0 in range(0, o_ref.shape[1], cols):
            o_ref[:, n0:n0 + cols] = partial_product(n0)

    @pl.when(kidx > 0)
    def _():
        for n0 in range(0, o_ref.shape[1], cols):
            o_ref[:, n0:n0 + cols] += partial_product(n0)

    @pl.when(kidx == nk - 1)
    def _():
        rows = 16
        for r0 in range(0, o_ref.shape[0], rows):
            r = alpha * x_ref[r0:r0 + rows, :] + o_ref[r0:r0 + rows, :]
            mu = jnp.mean(r, axis=-1, keepdims=True)
            d = r - mu
            var = jnp.mean(d * d, axis=-1, keepdims=True)
            y = d * lax.rsqrt(var + LN_EPS) * g_ref[...] + b_ref[...]
            o_ref[r0:r0 + rows, :] = y
            ob_ref[r0:r0 + rows, :] = y.astype(BF16)


def _outproj_ln(mix, wout, x, ln_g, ln_b, *, alpha, tm, tk):
    m, d = x.shape
    nk = D_MIX // tk
    return pl.pallas_call(
        functools.partial(_outproj_kernel, alpha=alpha, nk=nk),
        grid=(m // tm, nk),
        in_specs=[
            pl.BlockSpec((tm, tk), lambda i, kk: (i, kk)),
            pl.BlockSpec((tk, d), lambda i, kk: (kk, 0)),
            pl.BlockSpec((tm, d), lambda i, kk: (i, 0)),
            pl.BlockSpec((1, d), lambda i, kk: (0, 0)),
            pl.BlockSpec((1, d), lambda i, kk: (0, 0)),
        ],
        out_specs=[
            pl.BlockSpec((tm, d), lambda i, kk: (i, 0)),
            pl.BlockSpec((tm, d), lambda i, kk: (i, 0)),
        ],
        out_shape=[jax.ShapeDtypeStruct((m, d), F32), jax.ShapeDtypeStruct((m, d), BF16)],
        compiler_params=pltpu.CompilerParams(
            dimension_semantics=("parallel", "arbitrary"), vmem_limit_bytes=VMEM_LIMIT),
        name="outproj_ln",
    )(mix, wout, x, ln_g, ln_b)


def _pick(n, prefs):
    for p in prefs:
        if n % p == 0:
            return p
    raise ValueError(f"no tile of {prefs} divides {n}")


def _layer(x, xb, pos_col, consts, layer, wcat, wsmall, gla_w_a2, gla_b_a, dn_conv_w, dn_a_log, dn_dt_bias,
           ret_norm_g, gla_norm_g, dn_norm_g, w_out, ln_g, ln_b, *, b, t, alpha):
    inv2, lg, sel, pick = consts
    m, d = x.shape
    w2pad = jnp.concatenate(
        [gla_w_a2, jnp.zeros((LANE - GLA_RANK, GLA_HEADS * GLA_DK), gla_w_a2.dtype)], axis=0)

    proj, small = _inproj(xb, wcat, wsmall, layer, tm=_pick(m, (1024, 512, 256)), tn=_pick(N_CB * LANE, (1024,)))
    tb = _pick(t, (256,))
    mix = _retention(proj, pos_col, inv2, lg, ret_norm_g.reshape(1, -1), b=b, t=t, tb=tb)
    mix = _gla(proj, small, w2pad, gla_b_a.reshape(1, -1), gla_norm_g.reshape(1, -1), sel, mix,
               b=b, t=t, tb=tb)
    lane_pad = lambda p: jnp.zeros((1, LANE), F32).at[0, SM_A:SM_A + DN_HEADS].set(p)
    mix = _deltanet(proj, small, dn_conv_w, lane_pad(dn_a_log), lane_pad(dn_dt_bias), pick,
                    dn_norm_g.reshape(1, -1), mix, b=b, t=t, tb=_pick(t, (512, 256)), hg=DN_GROUP)
    return _outproj_ln(mix, w_out.astype(BF16), x, ln_g.reshape(1, -1), ln_b.reshape(1, -1),
                       alpha=alpha, tm=_pick(m, (512, 256)), tk=_pick(D_MIX, (256,)))


def kernel(x, positions, w_in, gla_w_a2, gla_b_a, dn_conv_w, dn_a_log, dn_dt_bias, ret_norm_g, gla_norm_g,
           dn_norm_g, w_out, ln_g, ln_b):
    b, t, d = x.shape
    depth = w_in.shape[0]
    alpha = float((2 * depth) ** 0.25)
    half = RET_DK // 2
    inv = ROPE_BASE ** (-jnp.arange(half, dtype=F32) / half)
    inv2 = jnp.concatenate([inv, inv]).reshape(1, LANE)
    lg = jnp.log(1.0 - jnp.power(2.0, -5.0 - jnp.arange(RET_HEADS, dtype=F32)))
    sel = (np.arange(SUB * LANE)[:, None] // LANE == np.arange(CHUNK)[None, :] % SUB)
    consts = (inv2, lg, jnp.asarray(sel, BF16), jnp.asarray(_dn_pick_table(DN_GROUP), BF16))
    pos_col = positions.reshape(b * t, 1)
    h = x.reshape(b * t, d)
    hb = h.astype(BF16)
    wcat, wsmall = _wprep(w_in, tr=_pick(d, (1024, 512, 256)), tn=_pick(N_CB_A * LANE, (1024,)))
    for l in range(depth):
        h, hb = _layer(h, hb, pos_col, consts, l, wcat, wsmall, gla_w_a2[l], gla_b_a[l], dn_conv_w[l], dn_a_log[l],
                       dn_dt_bias[l], ret_norm_g[l], gla_norm_g[l], dn_norm_g[l], w_out[l], ln_g[l], ln_b[l],
                       b=b, t=t, alpha=alpha)
    return h.reshape(b, t, d)
```

```python
import functools

import jax
import jax.numpy as jnp
import numpy as np
from jax import lax
from jax.experimental import pallas as pl
from jax.experimental.pallas import tpu as pltpu

CHUNK = 64
RET_HEADS, RET_DK, RET_DV = 4, 128, 256
GLA_HEADS, GLA_DK, GLA_DV, GLA_RANK, GLA_TAU = 4, 128, 256, 16, 16.0
DN_HEADS, DN_DK, DN_DV, CONV_K = 16, 128, 128, 4
ROPE_BASE = 10000.0
EPS = 1e-6
LN_EPS = 1e-5

LANE = 128
SUB = 16
DN_GROUP = 4

CB_RQ, CB_RK, CB_RV, CB_RG = 0, 4, 8, 16
CB_GQ, CB_GK, CB_GV, CB_GG = 24, 28, 32, 40
CB_DQ, CB_DK, CB_DV, CB_DG = 48, 64, 80, 96
N_CB_A = 48
N_CB = 112
SM_GA, SM_BETA, SM_A = 0, 16, 32
D_MIX = 4096

VMEM_LIMIT = 60 * 1024 * 1024

F32 = jnp.float32
BF16 = jnp.bfloat16
HI = lax.Precision.HIGHEST


def _mm(a, b):
    return jnp.dot(a.astype(BF16), b.astype(BF16), preferred_element_type=F32)


def _mm_nt(a, b):
    return lax.dot_general(a.astype(BF16), b.astype(BF16), (((1,), (1,)), ((), ())),
                           preferred_element_type=F32)


def _mm_tn(a, b):
    return lax.dot_general(a.astype(BF16), b.astype(BF16), (((0,), (0,)), ((), ())),
                           preferred_element_type=F32)


def _silu(x):
    return x * (1.0 / (1.0 + jnp.exp(-x)))


def _softplus(x):
    return jnp.maximum(x, 0.0) + jnp.log1p(jnp.exp(-jnp.abs(x)))


def _bf16_parts(x):
    hi = x.astype(BF16)
    r = x - hi.astype(F32)
    mid = r.astype(BF16)
    lo = (r - mid.astype(F32)).astype(BF16)
    return hi, mid, lo


def _inproj_kernel(x_ref, w_ref, ws_ref, p_ref, s_ref, *, tn):
    x = x_ref[...]
    acc = jnp.dot(x, w_ref[...], preferred_element_type=F32)
    for c in range(tn // LANE):
        p_ref[c] = acc[:, c * LANE:(c + 1) * LANE]

    @pl.when(pl.program_id(1) == 0)
    def _():
        s_ref[...] = jnp.dot(x, ws_ref[...], preferred_element_type=F32)


def _wprep_kernel(a_ref, ga_ref, bd_ref, o_ref, s_ref):
    o_ref[...] = a_ref[0].T.astype(BF16)

    @pl.when(pl.program_id(2) == 0)
    def _():
        pad = jnp.zeros((LANE - GLA_RANK - 2 * DN_HEADS, ga_ref.shape[2]), F32)
        s_ref[...] = jnp.concatenate([ga_ref[0], bd_ref[0], pad], axis=0).T.astype(BF16)


def _wprep(w_t, *, tk, tn):
    depth, _, d = w_t.shape
    n_a = N_CB_A * LANE
    dn0 = n_a + GLA_RANK
    small0 = dn0 + (N_CB - N_CB_A) * LANE
    row0 = lambda j: (j * (tn // GLA_RANK) + jnp.minimum(j // (n_a // tn), 1)) * GLA_RANK
    return pl.pallas_call(
        _wprep_kernel,
        grid=(depth, d // tk, N_CB * LANE // tn),
        in_specs=[
            pl.BlockSpec((pl.Element(1), pl.Element(tn), pl.Element(tk)), lambda l, kk, j: (l, row0(j), kk * tk)),
            pl.BlockSpec((pl.Element(1), pl.Element(GLA_RANK), pl.Element(tk)), lambda l, kk, j: (l, n_a, kk * tk)),
            pl.BlockSpec((pl.Element(1), pl.Element(2 * DN_HEADS), pl.Element(tk)),
                         lambda l, kk, j: (l, small0, kk * tk)),
        ],
        out_specs=[
            pl.BlockSpec((None, tk, tn), lambda l, kk, j: (l, kk, j)),
            pl.BlockSpec((None, tk, LANE), lambda l, kk, j: (l, kk, 0)),
        ],
        out_shape=[
            jax.ShapeDtypeStruct((depth, d, N_CB * LANE), BF16),
            jax.ShapeDtypeStruct((depth, d, LANE), BF16),
        ],
        compiler_params=pltpu.CompilerParams(
            dimension_semantics=("parallel", "parallel", "arbitrary"), vmem_limit_bytes=VMEM_LIMIT),
        name="wprep",
    )(w_t, w_t, w_t)


def _inproj(xb, wcat, wsmall, layer, *, tm, tn):
    m, d = xb.shape
    n = wcat.shape[2]
    grid = (m // tm, n // tn)
    return pl.pallas_call(
        functools.partial(_inproj_kernel, tn=tn),
        grid=grid,
        in_specs=[
            pl.BlockSpec((tm, d), lambda i, j: (i, 0)),
            pl.BlockSpec((None, d, tn), lambda i, j: (layer, 0, j)),
            pl.BlockSpec((None, d, LANE), lambda i, j: (layer, 0, 0)),
        ],
        out_specs=[
            pl.BlockSpec((tn // LANE, tm, LANE), lambda i, j: (j, i, 0)),
            pl.BlockSpec((tm, LANE), lambda i, j: (i, 0)),
        ],
        out_shape=[
            jax.ShapeDtypeStruct((n // LANE, m, LANE), F32),
            jax.ShapeDtypeStruct((m, LANE), F32),
        ],
        compiler_params=pltpu.CompilerParams(
            dimension_semantics=("parallel", "arbitrary"), vmem_limit_bytes=VMEM_LIMIT),
        name="inproj",
    )(xb, wcat, wsmall)


def _ret_kernel(lg_ref, q_ref, k_ref, v_ref, g_ref, pos_ref, inv_ref, ng_ref, o_ref, s_ref, *, tb):
    h = pl.program_id(1)

    @pl.when(pl.program_id(2) == 0)
    def _():
        s_ref[...] = jnp.zeros_like(s_ref)

    lg = lg_ref[h]
    ang = pos_ref[...].astype(F32) * inv_ref[...]
    lane = lax.broadcasted_iota(jnp.int32, (1, LANE), 1)
    cosv = jnp.cos(ang)
    sinv = jnp.sin(ang) * jnp.where(lane < LANE // 2, -1.0, 1.0)

    def rot(t):
        return t * cosv + pltpu.roll(t, LANE // 2, 1) * sinv

    q = rot(q_ref[...]) * (RET_DK ** -0.5)
    k = rot(k_ref[...])
    v = jnp.concatenate([v_ref[0], v_ref[1]], axis=1)

    ii = lax.broadcasted_iota(jnp.int32, (tb, tb), 0)
    jj = lax.broadcasted_iota(jnp.int32, (tb, tb), 1)
    causal = ii >= jj
    dmask = jnp.where(causal, jnp.exp(jnp.where(causal, (ii - jj).astype(F32), 0.0) * lg), 0.0)
    ri = lax.broadcasted_iota(jnp.int32, (tb, RET_DV), 0).astype(F32)
    xi = jnp.exp((ri + 1.0) * lg)
    zeta = jnp.exp((tb - 1.0 - ri[:, :RET_DK]) * lg)
    g_chunk = jnp.exp(jnp.full((1, RET_DV), tb * lg, F32))

    s = s_ref[...]
    scores = _mm_nt(q, k) * dmask
    o = _mm(scores, v) + _mm(q, s) * xi
    s_ref[...] = g_chunk * s + _mm_tn(k * zeta, v)

    mu = jnp.mean(o, axis=-1, keepdims=True)
    d = o - mu
    var = jnp.mean(d * d, axis=-1, keepdims=True)
    y = d * lax.rsqrt(var + EPS) * ng_ref[...]
    gate = jnp.concatenate([g_ref[0], g_ref[1]], axis=1)
    o_ref[...] = (y * _silu(gate)).astype(o_ref.dtype)


def _retention(proj, pos_col, inv2, lg, ng, *, b, t, tb):
    m = b * t
    nt = t // tb
    row = lambda bb, hh, tt: bb * nt + tt
    return pl.pallas_call(
        functools.partial(_ret_kernel, tb=tb),
        grid=(b, RET_HEADS, nt),
        in_specs=[
            pl.BlockSpec(memory_space=pltpu.SMEM),
            pl.BlockSpec((None, tb, LANE), lambda bb, hh, tt: (CB_RQ + hh, row(bb, hh, tt), 0)),
            pl.BlockSpec((None, tb, LANE), lambda bb, hh, tt: (CB_RK + hh, row(bb, hh, tt), 0)),
            pl.BlockSpec((2, tb, LANE), lambda bb, hh, tt: (CB_RV // 2 + hh, row(bb, hh, tt), 0)),
            pl.BlockSpec((2, tb, LANE), lambda bb, hh, tt: (CB_RG // 2 + hh, row(bb, hh, tt), 0)),
            pl.BlockSpec((tb, 1), lambda bb, hh, tt: (row(bb, hh, tt), 0)),
            pl.BlockSpec((1, LANE), lambda bb, hh, tt: (0, 0)),
            pl.BlockSpec((1, RET_DV), lambda bb, hh, tt: (0, hh)),
        ],
        out_specs=pl.BlockSpec((tb, RET_DV), lambda bb, hh, tt: (row(bb, hh, tt), hh)),
        out_shape=jax.ShapeDtypeStruct((m, D_MIX), BF16),
        scratch_shapes=[pltpu.VMEM((RET_DK, RET_DV), F32)],
        compiler_params=pltpu.CompilerParams(
            dimension_semantics=("parallel", "parallel", "arbitrary"), vmem_limit_bytes=VMEM_LIMIT),
        name="retention",
    )(lg, proj, proj, proj, proj, pos_col, inv2, ng)


def _gla_kernel(q_ref, k_ref, v_ref, g_ref, sm_ref, w2_ref, ba_ref, ng_ref, sel_ref, mix_in_ref,
                o_ref, st_ref, b_ref, p_ref, *, tb):
    del mix_in_ref

    @pl.when(pl.program_id(2) == 0)
    def _():
        st_ref[...] = jnp.zeros_like(st_ref)

    z = jnp.dot(sm_ref[...], w2_ref[...], precision=HI, preferred_element_type=F32) + ba_ref[...]
    log_a = (jnp.minimum(z, 0.0) - jnp.log1p(jnp.exp(-jnp.abs(z)))) / GLA_TAU
    ci = lax.broadcasted_iota(jnp.int32, (CHUNK, CHUNK), 0)
    cj = lax.broadcasted_iota(jnp.int32, (CHUNK, CHUNK), 1)
    nc = tb // CHUNK
    tri = jnp.broadcast_to(jnp.where(ci >= cj, 1.0, 0.0).astype(BF16)[None], (nc, CHUNK, CHUNK))
    b_ref[...] = sum(jnp.einsum("cij,cjk->cik", tri, p.reshape(nc, CHUNK, LANE), preferred_element_type=F32)
                     for p in _bf16_parts(log_a)).reshape(tb, LANE)

    rid = lax.broadcasted_iota(jnp.int32, (SUB, LANE), 0)
    same_sub = (ci // SUB) == (cj // SUB)
    nsub = CHUNK // SUB
    ng = ng_ref[...]
    sel = sel_ref[...]

    for c in range(tb // CHUNK):
        r0 = c * CHUNK
        qc = q_ref[r0:r0 + CHUNK, :] * (GLA_DK ** -0.5)
        kc = k_ref[r0:r0 + CHUNK, :]
        bc = b_ref[r0:r0 + CHUNK, :]
        vc = jnp.concatenate([v_ref[0, r0:r0 + CHUNK, :], v_ref[1, r0:r0 + CHUNK, :]], axis=1)

        for sb in range(nsub):
            qs = qc[sb * SUB:(sb + 1) * SUB]
            bs = bc[sb * SUB:(sb + 1) * SUB]
            for j in range(SUB):
                r = r0 + sb * SUB + j
                e = jnp.exp(jnp.where(rid >= j, bs - b_ref[r:r + 1, :], -jnp.inf))
                p_ref[sb * SUB:(sb + 1) * SUB, j * LANE:(j + 1) * LANE] = (
                    qs * k_ref[r:r + 1, :] * e).astype(BF16)
        diag = jnp.dot(p_ref[...], sel, preferred_element_type=F32)
        scores = jnp.where(same_sub, diag, 0.0)

        qparts, kparts = [], []
        for sb in range(1, nsub):
            lo = sb * SUB
            bref = bc[lo - 1:lo, :]
            qt = qc[lo:lo + SUB] * jnp.exp(bc[lo:lo + SUB] - bref)
            kt = kc[:lo] * jnp.exp(bref - bc[:lo])
            qparts.append(jnp.concatenate(
                [jnp.zeros((lo, LANE), F32), qt] +
                ([jnp.zeros((CHUNK - lo - SUB, LANE), F32)] if CHUNK - lo - SUB else []), axis=0))
            kparts.append(jnp.concatenate([kt, jnp.zeros((CHUNK - lo, LANE), F32)], axis=0))
        scores = scores + _mm_nt(jnp.concatenate(qparts, axis=1), jnp.concatenate(kparts, axis=1))

        st = st_ref[...]
        o = _mm(scores, vc) + _mm_nt(qc * jnp.exp(bc), st)
        b_last = bc[CHUNK - 1:CHUNK, :]
        st_ref[...] = st * jnp.exp(b_last) + _mm_tn(vc, kc * jnp.exp(b_last - bc))

        y = o * lax.rsqrt(jnp.mean(o * o, axis=-1, keepdims=True) + EPS) * ng
        gate = jnp.concatenate([g_ref[0, r0:r0 + CHUNK, :], g_ref[1, r0:r0 + CHUNK, :]], axis=1)
        o_ref[r0:r0 + CHUNK, :] = (y * _silu(gate)).astype(o_ref.dtype)


def _gla(proj, small, w2pad, ba, ng, sel, mix, *, b, t, tb):
    m = b * t
    nt = t // tb
    row = lambda bb, hh, tt: bb * nt + tt
    return pl.pallas_call(
        functools.partial(_gla_kernel, tb=tb),
        grid=(b, GLA_HEADS, nt),
        in_specs=[
            pl.BlockSpec((None, tb, LANE), lambda bb, hh, tt: (CB_GQ + hh, row(bb, hh, tt), 0)),
            pl.BlockSpec((None, tb, LANE), lambda bb, hh, tt: (CB_GK + hh, row(bb, hh, tt), 0)),
            pl.BlockSpec((2, tb, LANE), lambda bb, hh, tt: (CB_GV // 2 + hh, row(bb, hh, tt), 0)),
            pl.BlockSpec((2, tb, LANE), lambda bb, hh, tt: (CB_GG // 2 + hh, row(bb, hh, tt), 0)),
            pl.BlockSpec((tb, LANE), lambda bb, hh, tt: (row(bb, hh, tt), 0)),
            pl.BlockSpec((LANE, GLA_DK), lambda bb, hh, tt: (0, hh)),
            pl.BlockSpec((1, GLA_DK), lambda bb, hh, tt: (0, hh)),
            pl.BlockSpec((1, GLA_DV), lambda bb, hh, tt: (0, hh)),
            pl.BlockSpec((SUB * LANE, CHUNK), lambda bb, hh, tt: (0, 0)),
            pl.BlockSpec(memory_space=pl.ANY),
        ],
        out_specs=pl.BlockSpec((tb, GLA_DV), lambda bb, hh, tt: (row(bb, hh, tt), RET_HEADS + hh)),
        out_shape=jax.ShapeDtypeStruct((m, D_MIX), BF16),
        scratch_shapes=[
            pltpu.VMEM((GLA_DV, GLA_DK), F32),
            pltpu.VMEM((tb, GLA_DK), F32),
            pltpu.VMEM((CHUNK, SUB * LANE), BF16),
        ],
        input_output_aliases={9: 0},
        compiler_params=pltpu.CompilerParams(
            dimension_semantics=("parallel", "parallel", "arbitrary"), vmem_limit_bytes=VMEM_LIMIT),
        name="gla",
    )(proj, proj, proj, proj, small, w2pad, ba, ng, sel, mix)


def _bmm(a, b):
    return jnp.einsum("cij,cjk->cik", a.astype(BF16), b.astype(BF16), preferred_element_type=F32)


def _bmm_nt(a, b):
    return jnp.einsum("cik,cjk->cij", a.astype(BF16), b.astype(BF16), preferred_element_type=F32)


def _dn_kernel(q_ref, k_ref, v_ref, g_ref, sm_ref, wq_ref, wk_ref, wv_ref, ng_ref, alog_ref, dtb_ref, pick_ref,
               mix_in_ref, o_ref, s_ref, ext_ref, wp_ref, np_ref, sb_ref, qp_ref, qu_ref, egl_ref,
               *, tb, hg):
    del mix_in_ref
    nc = tb // CHUNK

    @pl.when(pl.program_id(2) == 0)
    def _():
        s_ref[...] = jnp.zeros_like(s_ref)
        ext_ref[:, 0:8, :] = jnp.zeros((3 * hg, 8, LANE), F32)

    sm = sm_ref[...]
    lane = lax.broadcasted_iota(jnp.int32, (1, LANE), 1)
    beta_all = 1.0 / (1.0 + jnp.exp(-sm))
    g_all = -jnp.exp(alog_ref[...]) * _softplus(sm + dtb_ref[...])
    ci = lax.broadcasted_iota(jnp.int32, (CHUNK, CHUNK), 0)
    cj = lax.broadcasted_iota(jnp.int32, (CHUNK, CHUNK), 1)
    tri = jnp.broadcast_to(jnp.where(ci >= cj, 1.0, 0.0).astype(BF16)[None], (nc, CHUNK, CHUNK))
    gc_all = sum(jnp.einsum("cij,cjk->cik", tri, p.reshape(nc, CHUNK, LANE), preferred_element_type=F32)
                 for p in _bf16_parts(g_all)).reshape(tb, LANE)
    parts = jnp.concatenate(_bf16_parts(jnp.where(lane < SM_A, beta_all, gc_all)), axis=1)
    bg = jnp.dot(parts, pick_ref[...], preferred_element_type=F32)

    lower = (ci >= cj)[None]
    strict = (ci > cj)[None]
    eye = jnp.where(ci == cj, 1.0, 0.0).astype(F32)[None]

    def conv(i, raw, w):
        ext_ref[i, 8:8 + tb, :] = raw
        out = ext_ref[i, 8 - (CONV_K - 1):8 - (CONV_K - 1) + tb, :] * w[0:1, :]
        for tap in range(1, CONV_K):
            lo = 8 - (CONV_K - 1) + tap
            out = out + ext_ref[i, lo:lo + tb, :] * w[tap:tap + 1, :]
        ext_ref[i, 0:8, :] = ext_ref[i, tb:tb + 8, :]
        return _silu(out)

    for hh in range(hg):
        cs = slice(hh * LANE, (hh + 1) * LANE)
        q = conv(3 * hh, q_ref[hh], wq_ref[:, cs])
        k = conv(3 * hh + 1, k_ref[hh], wk_ref[:, cs])
        v = conv(3 * hh + 2, v_ref[hh], wv_ref[:, cs])
        q = q * lax.rsqrt(jnp.sum(q * q, axis=-1, keepdims=True) + EPS) * (DN_DK ** -0.5)
        k = k * lax.rsqrt(jnp.sum(k * k, axis=-1, keepdims=True) + EPS)
        beta = bg[:, 2 * hh * LANE:(2 * hh + 1) * LANE]
        gc = bg[:, (2 * hh + 1) * LANE:(2 * hh + 2) * LANE]
        egc = jnp.exp(gc)

        gc3 = gc.reshape(nc, CHUNK, LANE)
        k3 = k.reshape(nc, CHUNK, LANE)
        q3 = q.reshape(nc, CHUNK, LANE)
        diff = gc3[:, :, :CHUNK] - jnp.swapaxes(gc3, 1, 2)[:, :CHUNK, :]
        decay = jnp.exp(jnp.where(lower, diff, -jnp.inf))
        a = jnp.where(strict, beta.reshape(nc, CHUNK, LANE)[:, :, :CHUNK] * _bmm_nt(k3, k3) * decay, 0.0)
        tinv = eye - a
        x = _bmm(a, a)
        for _ in range(4):
            tinv = tinv + _bmm(tinv, x)
            x = _bmm(x, x)
        tinv = tinv + _bmm(tinv, x)
        rhs = jnp.concatenate([v * beta, k * (beta * egc)], axis=1).reshape(nc, CHUNK, 2 * LANE)
        wu = _bmm(tinv, rhs).astype(BF16)
        g_last = gc3[:, CHUNK - 1:CHUNK, :]
        ke = (k3 * jnp.exp(g_last - gc3)).astype(BF16)
        qk = (_bmm_nt(q3, k3) * decay).astype(BF16)
        kt = jnp.einsum("cjk,cjd->ckd", ke, wu, preferred_element_type=F32)
        np_ref[hh] = kt[:, :, :DN_DV]
        wp_ref[hh] = kt[:, :, DN_DV:].astype(BF16)
        qt = jnp.einsum("cij,cjd->cid", qk, wu, preferred_element_type=F32)
        qu_ref[hh] = qt[:, :, :DN_DV]
        qp_ref[hh] = ((q * egc).reshape(nc, CHUNK, LANE) - qt[:, :, DN_DV:]).astype(BF16)
        egl_ref[hh] = jnp.exp(g_last)

    def chunk_step(c, carry):
        for hh in range(hg):
            s = s_ref[hh]
            sb = s.astype(BF16)
            sb_ref[hh, c] = sb
            s_ref[hh] = egl_ref[hh, c] * s + np_ref[hh, c] - jnp.dot(wp_ref[hh, c], sb, preferred_element_type=F32)
        return carry

    lax.fori_loop(0, nc, chunk_step, 0)

    for hh in range(hg):
        cs = slice(hh * LANE, (hh + 1) * LANE)
        o = (jnp.einsum("cik,ckd->cid", qp_ref[hh], sb_ref[hh], preferred_element_type=F32)
             + qu_ref[hh]).reshape(tb, LANE)
        y = o * lax.rsqrt(jnp.mean(o * o, axis=-1, keepdims=True) + EPS) * ng_ref[:, cs]
        o_ref[:, cs] = (y * _silu(g_ref[hh])).astype(o_ref.dtype)


def _deltanet(proj, small, conv_w, alog_row, dtb_row, pick, ng, mix, *, b, t, tb, hg):
    m = b * t
    nt = t // tb
    ngrp = DN_HEADS // hg
    row = lambda bb, gg, tt: bb * nt + tt
    nc = tb // CHUNK
    return pl.pallas_call(
        functools.partial(_dn_kernel, tb=tb, hg=hg),
        grid=(b, ngrp, nt),
        in_specs=[
            pl.BlockSpec((hg, tb, LANE), lambda bb, gg, tt: (CB_DQ // hg + gg, row(bb, gg, tt), 0)),
            pl.BlockSpec((hg, tb, LANE), lambda bb, gg, tt: (CB_DK // hg + gg, row(bb, gg, tt), 0)),
            pl.BlockSpec((hg, tb, LANE), lambda bb, gg, tt: (CB_DV // hg + gg, row(bb, gg, tt), 0)),
            pl.BlockSpec((hg, tb, LANE), lambda bb, gg, tt: (CB_DG // hg + gg, row(bb, gg, tt), 0)),
            pl.BlockSpec((tb, LANE), lambda bb, gg, tt: (row(bb, gg, tt), 0)),
            pl.BlockSpec((CONV_K, hg * LANE), lambda bb, gg, tt: (0, gg)),
            pl.BlockSpec((CONV_K, hg * LANE), lambda bb, gg, tt: (0, ngrp + gg)),
            pl.BlockSpec((CONV_K, hg * LANE), lambda bb, gg, tt: (0, 2 * ngrp + gg)),
            pl.BlockSpec((1, hg * DN_DV), lambda bb, gg, tt: (0, gg)),
            pl.BlockSpec((1, LANE), lambda bb, gg, tt: (0, 0)),
            pl.BlockSpec((1, LANE), lambda bb, gg, tt: (0, 0)),
            pl.BlockSpec((None, 3 * LANE, hg * 2 * LANE), lambda bb, gg, tt: (gg, 0, 0)),
            pl.BlockSpec(memory_space=pl.ANY),
        ],
        out_specs=pl.BlockSpec(
            (tb, hg * DN_DV),
            lambda bb, gg, tt: (row(bb, gg, tt), (RET_HEADS * RET_DV + GLA_HEADS * GLA_DV) // (hg * DN_DV) + gg)),
        out_shape=jax.ShapeDtypeStruct((m, D_MIX), BF16),
        scratch_shapes=[
            pltpu.VMEM((hg, DN_DK, DN_DV), F32),
            pltpu.VMEM((3 * hg, tb + 8, LANE), F32),
            pltpu.VMEM((hg, nc, DN_DK, DN_DK), BF16),
            pltpu.VMEM((hg, nc, DN_DK, DN_DV), F32),
            pltpu.VMEM((hg, nc, DN_DK, DN_DV), BF16),
            pltpu.VMEM((hg, nc, CHUNK, DN_DK), BF16),
            pltpu.VMEM((hg, nc, CHUNK, DN_DV), F32),
            pltpu.VMEM((hg, nc, 1, LANE), F32),
        ],
        input_output_aliases={12: 0},
        compiler_params=pltpu.CompilerParams(
            dimension_semantics=("parallel", "parallel", "arbitrary"), vmem_limit_bytes=VMEM_LIMIT),
        name="deltanet",
    )(proj, proj, proj, proj, small, conv_w, conv_w, conv_w, ng, alog_row, dtb_row, pick, mix)


def _dn_pick_table(hg):
    ngrp = DN_HEADS // hg
    src = np.arange(3 * LANE) % LANE
    col = np.arange(hg * 2 * LANE)
    hh, is_g = col // (2 * LANE), (col // LANE) % 2
    table = np.zeros((ngrp, 3 * LANE, hg * 2 * LANE), np.float32)
    for grp in range(ngrp):
        want = np.where(is_g == 1, SM_A, SM_BETA) + grp * hg + hh
        table[grp] = src[:, None] == want[None, :]
    return table


def _outproj_kernel(mix_ref, w_ref, x_ref, g_ref, b_ref, o_ref, ob_ref, *, alpha, nk):
    kidx = pl.program_id(1)

    cols = 512

    def partial_product(n0):
        return jnp.dot(mix_ref[...], w_ref[:, n0:n0 + cols], preferred_element_type=F32)

    @pl.when(kidx == 0)
    def _():
        for n0 in range(0, o_ref.shape[1], cols):
            o_ref[:, n0:n0 + cols] = partial_product(n0)

    @pl.when(kidx > 0)
    def _():
        for n0 in range(0, o_ref.shape[1], cols):
            o_ref[:, n0:n0 + cols] += partial_product(n0)

    @pl.when(kidx == nk - 1)
    def _():
        rows = 16
        for r0 in range(0, o_ref.shape[0], rows):
            r = alpha * x_ref[r0:r0 + rows, :] + o_ref[r0:r0 + rows, :]
            mu = jnp.mean(r, axis=-1, keepdims=True)
            d = r - mu
            var = jnp.mean(d * d, axis=-1, keepdims=True)
            y = d * lax.rsqrt(var + LN_EPS) * g_ref[...] + b_ref[...]
            o_ref[r0:r0 + rows, :] = y
            ob_ref[r0:r0 + rows, :] = y.astype(BF16)


def _outproj_ln(mix, wout, x, ln_g, ln_b, *, alpha, tm, tk):
    m, d = x.shape
    nk = D_MIX // tk
    return pl.pallas_call(
        functools.partial(_outproj_kernel, alpha=alpha, nk=nk),
        grid=(m // tm, nk),
        in_specs=[
            pl.BlockSpec((tm, tk), lambda i, kk: (i, kk)),
            pl.BlockSpec((tk, d), lambda i, kk: (kk, 0)),
            pl.BlockSpec((tm, d), lambda i, kk: (i, 0)),
            pl.BlockSpec((1, d), lambda i, kk: (0, 0)),
            pl.BlockSpec((1, d), lambda i, kk: (0, 0)),
        ],
        out_specs=[
            pl.BlockSpec((tm, d), lambda i, kk: (i, 0)),
            pl.BlockSpec((tm, d), lambda i, kk: (i, 0)),
        ],
        out_shape=[jax.ShapeDtypeStruct((m, d), F32), jax.ShapeDtypeStruct((m, d), BF16)],
        compiler_params=pltpu.CompilerParams(
            dimension_semantics=("parallel", "arbitrary"), vmem_limit_bytes=VMEM_LIMIT),
        name="outproj_ln",
    )(mix, wout, x, ln_g, ln_b)


def _pick(n, prefs):
    for p in prefs:
        if n % p == 0:
            return p
    raise ValueError(f"no tile of {prefs} divides {n}")


def _layer(x, xb, pos_col, consts, layer, wcat, wsmall, gla_w_a2, gla_b_a, dn_conv_w, dn_a_log, dn_dt_bias,
           ret_norm_g, gla_norm_g, dn_norm_g, w_out, ln_g, ln_b, *, b, t, alpha):
    inv2, lg, sel, pick = consts
    m, d = x.shape
    w2pad = jnp.concatenate(
        [gla_w_a2, jnp.zeros((LANE - GLA_RANK, GLA_HEADS * GLA_DK), gla_w_a2.dtype)], axis=0)

    proj, small = _inproj(xb, wcat, wsmall, layer, tm=_pick(m, (1024, 512, 256)), tn=_pick(N_CB * LANE, (1024,)))
    tb = _pick(t, (256,))
    mix = _retention(proj, pos_col, inv2, lg, ret_norm_g.reshape(1, -1), b=b, t=t, tb=tb)
    mix = _gla(proj, small, w2pad, gla_b_a.reshape(1, -1), gla_norm_g.reshape(1, -1), sel, mix,
               b=b, t=t, tb=tb)
    lane_pad = lambda p: jnp.zeros((1, LANE), F32).at[0, SM_A:SM_A + DN_HEADS].set(p)
    mix = _deltanet(proj, small, dn_conv_w, lane_pad(dn_a_log), lane_pad(dn_dt_bias), pick,
                    dn_norm_g.reshape(1, -1), mix, b=b, t=t, tb=_pick(t, (512, 256)), hg=DN_GROUP)
    return _outproj_ln(mix, w_out.astype(BF16), x, ln_g.reshape(1, -1), ln_b.reshape(1, -1),
                       alpha=alpha, tm=_pick(m, (512, 256)), tk=_pick(D_MIX, (512,)))


def kernel(x, positions, w_in, gla_w_a2, gla_b_a, dn_conv_w, dn_a_log, dn_dt_bias, ret_norm_g, gla_norm_g,
           dn_norm_g, w_out, ln_g, ln_b):
    b, t, d = x.shape
    depth = w_in.shape[0]
    alpha = float((2 * depth) ** 0.25)
    half = RET_DK // 2
    inv = ROPE_BASE ** (-jnp.arange(half, dtype=F32) / half)
    inv2 = jnp.concatenate([inv, inv]).reshape(1, LANE)
    lg = jnp.log(1.0 - jnp.power(2.0, -5.0 - jnp.arange(RET_HEADS, dtype=F32)))
    sel = (np.arange(SUB * LANE)[:, None] // LANE == np.arange(CHUNK)[None, :] % SUB)
    consts = (inv2, lg, jnp.asarray(sel, BF16), jnp.asarray(_dn_pick_table(DN_GROUP), BF16))
    pos_col = positions.reshape(b * t, 1)
    h = x.reshape(b * t, d)
    hb = h.astype(BF16)
    wcat, wsmall = _wprep(jnp.swapaxes(w_in, 1, 2), tk=_pick(d, (2048, 1024, 512, 256)),
                          tn=_pick(N_CB_A * LANE, (1024,)))
    for l in range(depth):
        h, hb = _layer(h, hb, pos_col, consts, l, wcat, wsmall, gla_w_a2[l], gla_b_a[l], dn_conv_w[l], dn_a_log[l],
                       dn_dt_bias[l], ret_norm_g[l], gla_norm_g[l], dn_norm_g[l], w_out[l], ln_g[l], ln_b[l],
                       b=b, t=t, alpha=alpha)
    return h.reshape(b, t, d)
```

```python
import functools

import jax
import jax.numpy as jnp
import numpy as np
from jax import lax
from jax.experimental import pallas as pl
from jax.experimental.pallas import tpu as pltpu

CHUNK = 64
RET_HEADS, RET_DK, RET_DV = 4, 128, 256
GLA_HEADS, GLA_DK, GLA_DV, GLA_RANK, GLA_TAU = 4, 128, 256, 16, 16.0
DN_HEADS, DN_DK, DN_DV, CONV_K = 16, 128, 128, 4
ROPE_BASE = 10000.0
EPS = 1e-6
LN_EPS = 1e-5

LANE = 128
SUB = 16
DN_GROUP = 8

CB_RQ, CB_RK, CB_RV, CB_RG = 0, 4, 8, 16
CB_GQ, CB_GK, CB_GV, CB_GG = 24, 28, 32, 40
CB_DQ, CB_DK, CB_DV, CB_DG = 48, 64, 80, 96
N_CB_A = 48
N_CB = 112
SM_GA, SM_BETA, SM_A = 0, 16, 32

VMEM_LIMIT = 60 * 1024 * 1024

F32 = jnp.float32
BF16 = jnp.bfloat16
HI = lax.Precision.HIGHEST


def _mm(a, b):
    return jnp.dot(a.astype(BF16), b.astype(BF16), preferred_element_type=F32)


def _mm_nt(a, b):
    return lax.dot_general(a.astype(BF16), b.astype(BF16), (((1,), (1,)), ((), ())),
                           preferred_element_type=F32)


def _mm_tn(a, b):
    return lax.dot_general(a.astype(BF16), b.astype(BF16), (((0,), (0,)), ((), ())),
                           preferred_element_type=F32)


def _silu(x):
    return x * (1.0 / (1.0 + jnp.exp(-x)))


def _softplus(x):
    return jnp.maximum(x, 0.0) + jnp.log1p(jnp.exp(-jnp.abs(x)))


def _bf16_parts(x):
    hi = x.astype(BF16)
    r = x - hi.astype(F32)
    mid = r.astype(BF16)
    lo = (r - mid.astype(F32)).astype(BF16)
    return hi, mid, lo


def _inproj_kernel(x_ref, w_ref, ws_ref, p_ref, s_ref, *, tn):
    x = x_ref[...]
    acc = jnp.dot(x, w_ref[...], preferred_element_type=F32)
    for c in range(tn // LANE):
        p_ref[c] = acc[:, c * LANE:(c + 1) * LANE]

    @pl.when(pl.program_id(1) == 0)
    def _():
        s_ref[...] = jnp.dot(x, ws_ref[...], preferred_element_type=F32)


def _wprep_kernel(a_ref, ga_ref, bd_ref, o_ref, s_ref):
    o_ref[...] = a_ref[0].T.astype(BF16)

    @pl.when(pl.program_id(2) == 0)
    def _():
        pad = jnp.zeros((LANE - GLA_RANK - 2 * DN_HEADS, ga_ref.shape[2]), F32)
        s_ref[...] = jnp.concatenate([ga_ref[0], bd_ref[0], pad], axis=0).T.astype(BF16)


def _wprep(w_t, *, tk, tn):
    depth, _, d = w_t.shape
    n_a = N_CB_A * LANE
    dn0 = n_a + GLA_RANK
    small0 = dn0 + (N_CB - N_CB_A) * LANE
    row0 = lambda j: (j * (tn // GLA_RANK) + jnp.minimum(j // (n_a // tn), 1)) * GLA_RANK
    return pl.pallas_call(
        _wprep_kernel,
        grid=(depth, d // tk, N_CB * LANE // tn),
        in_specs=[
            pl.BlockSpec((pl.Element(1), pl.Element(tn), pl.Element(tk)), lambda l, kk, j: (l, row0(j), kk * tk)),
            pl.BlockSpec((pl.Element(1), pl.Element(GLA_RANK), pl.Element(tk)), lambda l, kk, j: (l, n_a, kk * tk)),
            pl.BlockSpec((pl.Element(1), pl.Element(2 * DN_HEADS), pl.Element(tk)),
                         lambda l, kk, j: (l, small0, kk * tk)),
        ],
        out_specs=[
            pl.BlockSpec((None, tk, tn), lambda l, kk, j: (l, kk, j)),
            pl.BlockSpec((None, tk, LANE), lambda l, kk, j: (l, kk, 0)),
        ],
        out_shape=[
            jax.ShapeDtypeStruct((depth, d, N_CB * LANE), BF16),
            jax.ShapeDtypeStruct((depth, d, LANE), BF16),
        ],
        compiler_params=pltpu.CompilerParams(
            dimension_semantics=("parallel", "parallel", "arbitrary"), vmem_limit_bytes=VMEM_LIMIT),
        name="wprep",
    )(w_t, w_t, w_t)


def _inproj(xb, wcat, wsmall, layer, *, tm, tn):
    m, d = xb.shape
    n = wcat.shape[2]
    grid = (m // tm, n // tn)
    return pl.pallas_call(
        functools.partial(_inproj_kernel, tn=tn),
        grid=grid,
        in_specs=[
            pl.BlockSpec((tm, d), lambda i, j: (i, 0)),
            pl.BlockSpec((None, d, tn), lambda i, j: (layer, 0, j)),
            pl.BlockSpec((None, d, LANE), lambda i, j: (layer, 0, 0)),
        ],
        out_specs=[
            pl.BlockSpec((tn // LANE, tm, LANE), lambda i, j: (j, i, 0)),
            pl.BlockSpec((tm, LANE), lambda i, j: (i, 0)),
        ],
        out_shape=[
            jax.ShapeDtypeStruct((n // LANE, m, LANE), F32),
            jax.ShapeDtypeStruct((m, LANE), F32),
        ],
        compiler_params=pltpu.CompilerParams(
            dimension_semantics=("parallel", "arbitrary"), vmem_limit_bytes=VMEM_LIMIT),
        name="inproj",
    )(xb, wcat, wsmall)


def _ret_kernel(lg_ref, q_ref, k_ref, v_ref, g_ref, pos_ref, inv_ref, ng_ref, o_ref, s_ref, *, tb):
    @pl.when(pl.program_id(1) == 0)
    def _():
        s_ref[...] = jnp.zeros_like(s_ref)

    ang = pos_ref[...].astype(F32) * inv_ref[...]
    lane = lax.broadcasted_iota(jnp.int32, (1, LANE), 1)
    cosv = jnp.cos(ang)
    sinv = jnp.sin(ang) * jnp.where(lane < LANE // 2, -1.0, 1.0)

    def rot(t):
        return t * cosv + pltpu.roll(t, LANE // 2, 1) * sinv

    ii = lax.broadcasted_iota(jnp.int32, (tb, tb), 0)
    jj = lax.broadcasted_iota(jnp.int32, (tb, tb), 1)
    causal = ii >= jj
    dist = jnp.where(causal, (ii - jj).astype(F32), 0.0)
    ri = lax.broadcasted_iota(jnp.int32, (tb, RET_DV), 0).astype(F32)

    for h in range(RET_HEADS):
        lg = lg_ref[h]
        q = rot(q_ref[h]) * (RET_DK ** -0.5)
        k = rot(k_ref[h])
        v = jnp.concatenate([v_ref[2 * h], v_ref[2 * h + 1]], axis=1)
        dmask = jnp.where(causal, jnp.exp(dist * lg), 0.0)
        xi = jnp.exp((ri + 1.0) * lg)
        zeta = jnp.exp((tb - 1.0 - ri[:, :RET_DK]) * lg)
        g_chunk = jnp.exp(jnp.full((1, RET_DV), tb * lg, F32))

        s = s_ref[h]
        scores = _mm_nt(q, k) * dmask
        o = _mm(scores, v) + _mm(q, s) * xi
        s_ref[h] = g_chunk * s + _mm_tn(k * zeta, v)

        mu = jnp.mean(o, axis=-1, keepdims=True)
        d = o - mu
        var = jnp.mean(d * d, axis=-1, keepdims=True)
        cs = slice(h * RET_DV, (h + 1) * RET_DV)
        y = d * lax.rsqrt(var + EPS) * ng_ref[:, cs]
        gate = jnp.concatenate([g_ref[2 * h], g_ref[2 * h + 1]], axis=1)
        o_ref[:, cs] = (y * _silu(gate)).astype(o_ref.dtype)


def _retention(proj, pos_col, inv2, lg, ng, *, b, t, tb):
    m = b * t
    nt = t // tb
    row = lambda bb, tt: bb * nt + tt
    nh, nv = RET_HEADS, RET_HEADS * RET_DV // LANE
    return pl.pallas_call(
        functools.partial(_ret_kernel, tb=tb),
        grid=(b, nt),
        in_specs=[
            pl.BlockSpec(memory_space=pltpu.SMEM),
            pl.BlockSpec((nh, tb, LANE), lambda bb, tt: (CB_RQ // nh, row(bb, tt), 0)),
            pl.BlockSpec((nh, tb, LANE), lambda bb, tt: (CB_RK // nh, row(bb, tt), 0)),
            pl.BlockSpec((nv, tb, LANE), lambda bb, tt: (CB_RV // nv, row(bb, tt), 0)),
            pl.BlockSpec((nv, tb, LANE), lambda bb, tt: (CB_RG // nv, row(bb, tt), 0)),
            pl.BlockSpec((tb, 1), lambda bb, tt: (row(bb, tt), 0)),
            pl.BlockSpec((1, LANE), lambda bb, tt: (0, 0)),
            pl.BlockSpec((1, nh * RET_DV), lambda bb, tt: (0, 0)),
        ],
        out_specs=pl.BlockSpec((tb, nh * RET_DV), lambda bb, tt: (row(bb, tt), 0)),
        out_shape=jax.ShapeDtypeStruct((m, nh * RET_DV), BF16),
        scratch_shapes=[pltpu.VMEM((nh, RET_DK, RET_DV), F32)],
        compiler_params=pltpu.CompilerParams(
            dimension_semantics=("parallel", "arbitrary"), vmem_limit_bytes=VMEM_LIMIT),
        name="retention",
    )(lg, proj, proj, proj, proj, pos_col, inv2, ng)


def _gla_kernel(q_ref, k_ref, v_ref, g_ref, sm_ref, w2_ref, ba_ref, ng_ref, sel_ref,
                o_ref, st_ref, b_ref, p_ref, *, tb):
    @pl.when(pl.program_id(2) == 0)
    def _():
        st_ref[...] = jnp.zeros_like(st_ref)

    z = jnp.dot(sm_ref[...], w2_ref[...], precision=HI, preferred_element_type=F32) + ba_ref[...]
    log_a = (jnp.minimum(z, 0.0) - jnp.log1p(jnp.exp(-jnp.abs(z)))) / GLA_TAU
    ci = lax.broadcasted_iota(jnp.int32, (CHUNK, CHUNK), 0)
    cj = lax.broadcasted_iota(jnp.int32, (CHUNK, CHUNK), 1)
    nc = tb // CHUNK
    tri = jnp.broadcast_to(jnp.where(ci >= cj, 1.0, 0.0).astype(BF16)[None], (nc, CHUNK, CHUNK))
    b_ref[...] = sum(jnp.einsum("cij,cjk->cik", tri, p.reshape(nc, CHUNK, LANE), preferred_element_type=F32)
                     for p in _bf16_parts(log_a)).reshape(tb, LANE)

    rid = lax.broadcasted_iota(jnp.int32, (SUB, LANE), 0)
    same_sub = (ci // SUB) == (cj // SUB)
    nsub = CHUNK // SUB
    ng = ng_ref[...]
    sel = sel_ref[...]

    for c in range(tb // CHUNK):
        r0 = c * CHUNK
        qc = q_ref[r0:r0 + CHUNK, :] * (GLA_DK ** -0.5)
        kc = k_ref[r0:r0 + CHUNK, :]
        bc = b_ref[r0:r0 + CHUNK, :]
        vc = jnp.concatenate([v_ref[0, r0:r0 + CHUNK, :], v_ref[1, r0:r0 + CHUNK, :]], axis=1)

        for sb in range(nsub):
            qs = qc[sb * SUB:(sb + 1) * SUB]
            bs = bc[sb * SUB:(sb + 1) * SUB]
            for j in range(SUB):
                r = r0 + sb * SUB + j
                e = jnp.exp(jnp.where(rid >= j, bs - b_ref[r:r + 1, :], -jnp.inf))
                p_ref[sb * SUB:(sb + 1) * SUB, j * LANE:(j + 1) * LANE] = (
                    qs * k_ref[r:r + 1, :] * e).astype(BF16)
        diag = jnp.dot(p_ref[...], sel, preferred_element_type=F32)
        scores = jnp.where(same_sub, diag, 0.0)

        qparts, kparts = [], []
        for sb in range(1, nsub):
            lo = sb * SUB
            bref = bc[lo - 1:lo, :]
            qt = qc[lo:lo + SUB] * jnp.exp(bc[lo:lo + SUB] - bref)
            kt = kc[:lo] * jnp.exp(bref - bc[:lo])
            qparts.append(jnp.concatenate(
                [jnp.zeros((lo, LANE), F32), qt] +
                ([jnp.zeros((CHUNK - lo - SUB, LANE), F32)] if CHUNK - lo - SUB else []), axis=0))
            kparts.append(jnp.concatenate([kt, jnp.zeros((CHUNK - lo, LANE), F32)], axis=0))
        scores = scores + _mm_nt(jnp.concatenate(qparts, axis=1), jnp.concatenate(kparts, axis=1))

        st = st_ref[...]
        o = _mm(scores, vc) + _mm_nt(qc * jnp.exp(bc), st)
        b_last = bc[CHUNK - 1:CHUNK, :]
        st_ref[...] = st * jnp.exp(b_last) + _mm_tn(vc, kc * jnp.exp(b_last - bc))

        y = o * lax.rsqrt(jnp.mean(o * o, axis=-1, keepdims=True) + EPS) * ng
        gate = jnp.concatenate([g_ref[0, r0:r0 + CHUNK, :], g_ref[1, r0:r0 + CHUNK, :]], axis=1)
        o_ref[r0:r0 + CHUNK, :] = (y * _silu(gate)).astype(o_ref.dtype)


def _gla(proj, small, w2pad, ba, ng, sel, *, b, t, tb):
    m = b * t
    nt = t // tb
    row = lambda bb, hh, tt: bb * nt + tt
    return pl.pallas_call(
        functools.partial(_gla_kernel, tb=tb),
        grid=(b, GLA_HEADS, nt),
        in_specs=[
            pl.BlockSpec((None, tb, LANE), lambda bb, hh, tt: (CB_GQ + hh, row(bb, hh, tt), 0)),
            pl.BlockSpec((None, tb, LANE), lambda bb, hh, tt: (CB_GK + hh, row(bb, hh, tt), 0)),
            pl.BlockSpec((2, tb, LANE), lambda bb, hh, tt: (CB_GV // 2 + hh, row(bb, hh, tt), 0)),
            pl.BlockSpec((2, tb, LANE), lambda bb, hh, tt: (CB_GG // 2 + hh, row(bb, hh, tt), 0)),
            pl.BlockSpec((tb, LANE), lambda bb, hh, tt: (row(bb, hh, tt), 0)),
            pl.BlockSpec((LANE, GLA_DK), lambda bb, hh, tt: (0, hh)),
            pl.BlockSpec((1, GLA_DK), lambda bb, hh, tt: (0, hh)),
            pl.BlockSpec((1, GLA_DV), lambda bb, hh, tt: (0, hh)),
            pl.BlockSpec((SUB * LANE, CHUNK), lambda bb, hh, tt: (0, 0)),
        ],
        out_specs=pl.BlockSpec((tb, GLA_DV), lambda bb, hh, tt: (row(bb, hh, tt), hh)),
        out_shape=jax.ShapeDtypeStruct((m, GLA_HEADS * GLA_DV), BF16),
        scratch_shapes=[
            pltpu.VMEM((GLA_DV, GLA_DK), F32),
            pltpu.VMEM((tb, GLA_DK), F32),
            pltpu.VMEM((CHUNK, SUB * LANE), BF16),
        ],
        compiler_params=pltpu.CompilerParams(
            dimension_semantics=("parallel", "parallel", "arbitrary"), vmem_limit_bytes=VMEM_LIMIT),
        name="gla",
    )(proj, proj, proj, proj, small, w2pad, ba, ng, sel)


def _bmm(a, b):
    return jnp.einsum("cij,cjk->cik", a.astype(BF16), b.astype(BF16), preferred_element_type=F32)


def _bmm_nt(a, b):
    return jnp.einsum("cik,cjk->cij", a.astype(BF16), b.astype(BF16), preferred_element_type=F32)


def _dn_kernel(q_ref, k_ref, v_ref, g_ref, sm_ref, wq_ref, wk_ref, wv_ref, ng_ref, alog_ref, dtb_ref, pick_ref,
               o_ref, s_ref, ext_ref, wp_ref, np_ref, sb_ref, qp_ref, qu_ref, egl_ref, *, tb, hg):
    nc = tb // CHUNK

    @pl.when(pl.program_id(2) == 0)
    def _():
        s_ref[...] = jnp.zeros_like(s_ref)
        ext_ref[:, 0:8, :] = jnp.zeros((3 * hg, 8, LANE), F32)

    sm = sm_ref[...]
    lane = lax.broadcasted_iota(jnp.int32, (1, LANE), 1)
    beta_all = 1.0 / (1.0 + jnp.exp(-sm))
    g_all = -jnp.exp(alog_ref[...]) * _softplus(sm + dtb_ref[...])
    ci = lax.broadcasted_iota(jnp.int32, (CHUNK, CHUNK), 0)
    cj = lax.broadcasted_iota(jnp.int32, (CHUNK, CHUNK), 1)
    tri = jnp.broadcast_to(jnp.where(ci >= cj, 1.0, 0.0).astype(BF16)[None], (nc, CHUNK, CHUNK))
    gc_all = sum(jnp.einsum("cij,cjk->cik", tri, p.reshape(nc, CHUNK, LANE), preferred_element_type=F32)
                 for p in _bf16_parts(g_all)).reshape(tb, LANE)
    parts = jnp.concatenate(_bf16_parts(jnp.where(lane < SM_A, beta_all, gc_all)), axis=1)
    bg = jnp.dot(parts, pick_ref[...], preferred_element_type=F32)

    lower = (ci >= cj)[None]
    strict = (ci > cj)[None]
    eye = jnp.where(ci == cj, 1.0, 0.0).astype(F32)[None]

    def conv(i, raw, w):
        ext_ref[i, 8:8 + tb, :] = raw
        out = ext_ref[i, 8 - (CONV_K - 1):8 - (CONV_K - 1) + tb, :] * w[0:1, :]
        for tap in range(1, CONV_K):
            lo = 8 - (CONV_K - 1) + tap
            out = out + ext_ref[i, lo:lo + tb, :] * w[tap:tap + 1, :]
        ext_ref[i, 0:8, :] = ext_ref[i, tb:tb + 8, :]
        return _silu(out)

    for hh in range(hg):
        cs = slice(hh * LANE, (hh + 1) * LANE)
        q = conv(3 * hh, q_ref[hh], wq_ref[:, cs])
        k = conv(3 * hh + 1, k_ref[hh], wk_ref[:, cs])
        v = conv(3 * hh + 2, v_ref[hh], wv_ref[:, cs])
        q = q * lax.rsqrt(jnp.sum(q * q, axis=-1, keepdims=True) + EPS) * (DN_DK ** -0.5)
        k = k * lax.rsqrt(jnp.sum(k * k, axis=-1, keepdims=True) + EPS)
        beta = bg[:, 2 * hh * LANE:(2 * hh + 1) * LANE]
        gc = bg[:, (2 * hh + 1) * LANE:(2 * hh + 2) * LANE]
        egc = jnp.exp(gc)

        gc3 = gc.reshape(nc, CHUNK, LANE)
        k3 = k.reshape(nc, CHUNK, LANE)
        q3 = q.reshape(nc, CHUNK, LANE)
        diff = gc3[:, :, :CHUNK] - jnp.swapaxes(gc3, 1, 2)[:, :CHUNK, :]
        decay = jnp.exp(jnp.where(lower, diff, -jnp.inf))
        a = jnp.where(strict, beta.reshape(nc, CHUNK, LANE)[:, :, :CHUNK] * _bmm_nt(k3, k3) * decay, 0.0)
        tinv = eye - a
        x = _bmm(a, a)
        for _ in range(4):
            tinv = tinv + _bmm(tinv, x)
            x = _bmm(x, x)
        tinv = tinv + _bmm(tinv, x)
        rhs = jnp.concatenate([v * beta, k * (beta * egc)], axis=1).reshape(nc, CHUNK, 2 * LANE)
        wu = _bmm(tinv, rhs).astype(BF16)
        g_last = gc3[:, CHUNK - 1:CHUNK, :]
        ke = (k3 * jnp.exp(g_last - gc3)).astype(BF16)
        qk = (_bmm_nt(q3, k3) * decay).astype(BF16)
        kt = jnp.einsum("cjk,cjd->ckd", ke, wu, preferred_element_type=F32)
        np_ref[hh] = kt[:, :, :DN_DV]
        wp_ref[hh] = kt[:, :, DN_DV:].astype(BF16)
        qt = jnp.einsum("cij,cjd->cid", qk, wu, preferred_element_type=F32)
        qu_ref[hh] = qt[:, :, :DN_DV]
        qp_ref[hh] = ((q * egc).reshape(nc, CHUNK, LANE) - qt[:, :, DN_DV:]).astype(BF16)
        egl_ref[hh] = jnp.exp(g_last)

    def chunk_step(c, carry):
        for hh in range(hg):
            s = s_ref[hh]
            sb = s.astype(BF16)
            sb_ref[hh, c] = sb
            s_ref[hh] = egl_ref[hh, c] * s + np_ref[hh, c] - jnp.dot(wp_ref[hh, c], sb, preferred_element_type=F32)
        return carry

    lax.fori_loop(0, nc, chunk_step, 0)

    for hh in range(hg):
        cs = slice(hh * LANE, (hh + 1) * LANE)
        o = (jnp.einsum("cik,ckd->cid", qp_ref[hh], sb_ref[hh], preferred_element_type=F32)
             + qu_ref[hh]).reshape(tb, LANE)
        y = o * lax.rsqrt(jnp.mean(o * o, axis=-1, keepdims=True) + EPS) * ng_ref[:, cs]
        o_ref[:, cs] = (y * _silu(g_ref[hh])).astype(o_ref.dtype)


def _deltanet(proj, small, conv_w, alog_row, dtb_row, pick, ng, *, b, t, tb, hg):
    m = b * t
    nt = t // tb
    ngrp = DN_HEADS // hg
    row = lambda bb, gg, tt: bb * nt + tt
    nc = tb // CHUNK
    return pl.pallas_call(
        functools.partial(_dn_kernel, tb=tb, hg=hg),
        grid=(b, ngrp, nt),
        in_specs=[
            pl.BlockSpec((hg, tb, LANE), lambda bb, gg, tt: (CB_DQ // hg + gg, row(bb, gg, tt), 0)),
            pl.BlockSpec((hg, tb, LANE), lambda bb, gg, tt: (CB_DK // hg + gg, row(bb, gg, tt), 0)),
            pl.BlockSpec((hg, tb, LANE), lambda bb, gg, tt: (CB_DV // hg + gg, row(bb, gg, tt), 0)),
            pl.BlockSpec((hg, tb, LANE), lambda bb, gg, tt: (CB_DG // hg + gg, row(bb, gg, tt), 0)),
            pl.BlockSpec((tb, LANE), lambda bb, gg, tt: (row(bb, gg, tt), 0)),
            pl.BlockSpec((CONV_K, hg * LANE), lambda bb, gg, tt: (0, gg)),
            pl.BlockSpec((CONV_K, hg * LANE), lambda bb, gg, tt: (0, ngrp + gg)),
            pl.BlockSpec((CONV_K, hg * LANE), lambda bb, gg, tt: (0, 2 * ngrp + gg)),
            pl.BlockSpec((1, hg * DN_DV), lambda bb, gg, tt: (0, gg)),
            pl.BlockSpec((1, LANE), lambda bb, gg, tt: (0, 0)),
            pl.BlockSpec((1, LANE), lambda bb, gg, tt: (0, 0)),
            pl.BlockSpec((None, 3 * LANE, hg * 2 * LANE), lambda bb, gg, tt: (gg, 0, 0)),
        ],
        out_specs=pl.BlockSpec((tb, hg * DN_DV), lambda bb, gg, tt: (row(bb, gg, tt), gg)),
        out_shape=jax.ShapeDtypeStruct((m, DN_HEADS * DN_DV), BF16),
        scratch_shapes=[
            pltpu.VMEM((hg, DN_DK, DN_DV), F32),
            pltpu.VMEM((3 * hg, tb + 8, LANE), F32),
            pltpu.VMEM((hg, nc, DN_DK, DN_DK), BF16),
            pltpu.VMEM((hg, nc, DN_DK, DN_DV), F32),
            pltpu.VMEM((hg, nc, DN_DK, DN_DV), BF16),
            pltpu.VMEM((hg, nc, CHUNK, DN_DK), BF16),
            pltpu.VMEM((hg, nc, CHUNK, DN_DV), F32),
            pltpu.VMEM((hg, nc, 1, LANE), F32),
        ],
        compiler_params=pltpu.CompilerParams(
            dimension_semantics=("parallel", "parallel", "arbitrary"), vmem_limit_bytes=VMEM_LIMIT),
        name="deltanet",
    )(proj, proj, proj, proj, small, conv_w, conv_w, conv_w, ng, alog_row, dtb_row, pick)


def _dn_pick_table(hg):
    ngrp = DN_HEADS // hg
    src = np.arange(3 * LANE) % LANE
    col = np.arange(hg * 2 * LANE)
    hh, is_g = col // (2 * LANE), (col // LANE) % 2
    table = np.zeros((ngrp, 3 * LANE, hg * 2 * LANE), np.float32)
    for grp in range(ngrp):
        want = np.where(is_g == 1, SM_A, SM_BETA) + grp * hg + hh
        table[grp] = src[:, None] == want[None, :]
    return table


def _outproj_kernel(ret_ref, gla_ref, dn_ref, w_ref, x_ref, g_ref, b_ref, o_ref, ob_ref,
                    *, alpha, nk, k_gla, k_dn):
    kidx = pl.program_id(1)

    cols = 512

    def accumulate(mix_ref, first):
        for n0 in range(0, o_ref.shape[1], cols):
            p = jnp.dot(mix_ref[...], w_ref[:, n0:n0 + cols], preferred_element_type=F32)
            if first:
                o_ref[:, n0:n0 + cols] = p
            else:
                o_ref[:, n0:n0 + cols] += p

    pl.when(kidx == 0)(lambda: accumulate(ret_ref, True))
    pl.when((kidx > 0) & (kidx < k_gla))(lambda: accumulate(ret_ref, False))
    pl.when((kidx >= k_gla) & (kidx < k_dn))(lambda: accumulate(gla_ref, False))
    pl.when(kidx >= k_dn)(lambda: accumulate(dn_ref, False))

    @pl.when(kidx == nk - 1)
    def _():
        rows = 16
        for r0 in range(0, o_ref.shape[0], rows):
            r = alpha * x_ref[r0:r0 + rows, :] + o_ref[r0:r0 + rows, :]
            mu = jnp.mean(r, axis=-1, keepdims=True)
            d = r - mu
            var = jnp.mean(d * d, axis=-1, keepdims=True)
            y = d * lax.rsqrt(var + LN_EPS) * g_ref[...] + b_ref[...]
            o_ref[r0:r0 + rows, :] = y
            ob_ref[r0:r0 + rows, :] = y.astype(BF16)


def _outproj_ln(ret, gla, dn, wout, x, ln_g, ln_b, *, alpha, tm, tk):
    m, d = x.shape
    k_gla = ret.shape[1] // tk
    k_dn = k_gla + gla.shape[1] // tk
    nk = k_dn + dn.shape[1] // tk
    return pl.pallas_call(
        functools.partial(_outproj_kernel, alpha=alpha, nk=nk, k_gla=k_gla, k_dn=k_dn),
        grid=(m // tm, nk),
        in_specs=[
            pl.BlockSpec((tm, tk), lambda i, kk: (i, jnp.minimum(kk, k_gla - 1))),
            pl.BlockSpec((tm, tk), lambda i, kk: (i, jnp.clip(kk - k_gla, 0, k_dn - k_gla - 1))),
            pl.BlockSpec((tm, tk), lambda i, kk: (i, jnp.clip(kk - k_dn, 0, nk - k_dn - 1))),
            pl.BlockSpec((tk, d), lambda i, kk: (kk, 0)),
            pl.BlockSpec((tm, d), lambda i, kk: (i, 0)),
            pl.BlockSpec((1, d), lambda i, kk: (0, 0)),
            pl.BlockSpec((1, d), lambda i, kk: (0, 0)),
        ],
        out_specs=[
            pl.BlockSpec((tm, d), lambda i, kk: (i, 0)),
            pl.BlockSpec((tm, d), lambda i, kk: (i, 0)),
        ],
        out_shape=[jax.ShapeDtypeStruct((m, d), F32), jax.ShapeDtypeStruct((m, d), BF16)],
        compiler_params=pltpu.CompilerParams(
            dimension_semantics=("parallel", "arbitrary"), vmem_limit_bytes=VMEM_LIMIT),
        name="outproj_ln",
    )(ret, gla, dn, wout, x, ln_g, ln_b)


def _pick(n, prefs):
    for p in prefs:
        if n % p == 0:
            return p
    raise ValueError(f"no tile of {prefs} divides {n}")


def _layer(x, xb, pos_col, consts, layer, wcat, wsmall, gla_w_a2, gla_b_a, dn_conv_w, dn_a_log, dn_dt_bias,
           ret_norm_g, gla_norm_g, dn_norm_g, w_out, ln_g, ln_b, *, b, t, alpha):
    inv2, lg, sel, pick = consts
    m, d = x.shape
    w2pad = jnp.concatenate(
        [gla_w_a2, jnp.zeros((LANE - GLA_RANK, GLA_HEADS * GLA_DK), gla_w_a2.dtype)], axis=0)

    proj, small = _inproj(xb, wcat, wsmall, layer, tm=_pick(m, (1024, 512, 256)), tn=_pick(N_CB * LANE, (1024,)))
    tb = _pick(t, (256,))
    ret = _retention(proj, pos_col, inv2, lg, ret_norm_g.reshape(1, -1), b=b, t=t, tb=tb)
    gla = _gla(proj, small, w2pad, gla_b_a.reshape(1, -1), gla_norm_g.reshape(1, -1), sel, b=b, t=t, tb=tb)
    lane_pad = lambda p: jnp.zeros((1, LANE), F32).at[0, SM_A:SM_A + DN_HEADS].set(p)
    dn = _deltanet(proj, small, dn_conv_w, lane_pad(dn_a_log), lane_pad(dn_dt_bias), pick,
                   dn_norm_g.reshape(1, -1), b=b, t=t, tb=_pick(t, (512, 256)), hg=DN_GROUP)
    return _outproj_ln(ret, gla, dn, w_out.astype(BF16), x, ln_g.reshape(1, -1), ln_b.reshape(1, -1),
                       alpha=alpha, tm=_pick(m, (512, 256)), tk=512)


def kernel(x, positions, w_in, gla_w_a2, gla_b_a, dn_conv_w, dn_a_log, dn_dt_bias, ret_norm_g, gla_norm_g,
           dn_norm_g, w_out, ln_g, ln_b):
    b, t, d = x.shape
    depth = w_in.shape[0]
    alpha = float((2 * depth) ** 0.25)
    half = RET_DK // 2
    inv = ROPE_BASE ** (-jnp.arange(half, dtype=F32) / half)
    inv2 = jnp.concatenate([inv, inv]).reshape(1, LANE)
    lg = jnp.log(1.0 - jnp.power(2.0, -5.0 - jnp.arange(RET_HEADS, dtype=F32)))
    sel = (np.arange(SUB * LANE)[:, None] // LANE == np.arange(CHUNK)[None, :] % SUB)
    consts = (inv2, lg, jnp.asarray(sel, BF16), jnp.asarray(_dn_pick_table(DN_GROUP), BF16))
    pos_col = positions.reshape(b * t, 1)
    h = x.reshape(b * t, d)
    hb = h.astype(BF16)
    wcat, wsmall = _wprep(jnp.swapaxes(w_in, 1, 2), tk=_pick(d, (2048, 1024, 512, 256)),
                          tn=_pick(N_CB_A * LANE, (1024,)))
    for l in range(depth):
        h, hb = _layer(h, hb, pos_col, consts, l, wcat, wsmall, gla_w_a2[l], gla_b_a[l], dn_conv_w[l], dn_a_log[l],
                       dn_dt_bias[l], ret_norm_g[l], gla_norm_g[l], dn_norm_g[l], w_out[l], ln_g[l], ln_b[l],
                       b=b, t=t, alpha=alpha)
    return h.reshape(b, t, d)
```

```python
import functools

import jax
import jax.numpy as jnp
import numpy as np
from jax import lax
from jax.experimental import pallas as pl
from jax.experimental.pallas import tpu as pltpu

CHUNK = 64
RET_HEADS, RET_DK, RET_DV = 4, 128, 256
GLA_HEADS, GLA_DK, GLA_DV, GLA_RANK, GLA_TAU = 4, 128, 256, 16, 16.0
DN_HEADS, DN_DK, DN_DV, CONV_K = 16, 128, 128, 4
ROPE_BASE = 10000.0
EPS = 1e-6
LN_EPS = 1e-5

LANE = 128
SUB = 16
DN_GROUP = 8

CB_RQ, CB_RK, CB_RV, CB_RG = 0, 4, 8, 16
CB_GQ, CB_GK, CB_GV, CB_GG = 24, 28, 32, 40
CB_DQ, CB_DK, CB_DV, CB_DG = 48, 64, 80, 96
N_CB_A = 48
N_CB = 112
SM_GA, SM_BETA, SM_A = 0, 16, 32

VMEM_LIMIT = 60 * 1024 * 1024

F32 = jnp.float32
BF16 = jnp.bfloat16
HI = lax.Precision.HIGHEST


def _mm(a, b):
    return jnp.dot(a.astype(BF16), b.astype(BF16), preferred_element_type=F32)


def _mm_nt(a, b):
    return lax.dot_general(a.astype(BF16), b.astype(BF16), (((1,), (1,)), ((), ())),
                           preferred_element_type=F32)


def _mm_tn(a, b):
    return lax.dot_general(a.astype(BF16), b.astype(BF16), (((0,), (0,)), ((), ())),
                           preferred_element_type=F32)


def _silu(x):
    return x * (1.0 / (1.0 + jnp.exp(-x)))


def _softplus(x):
    return jnp.maximum(x, 0.0) + jnp.log1p(jnp.exp(-jnp.abs(x)))


def _bf16_parts(x):
    hi = x.astype(BF16)
    r = x - hi.astype(F32)
    mid = r.astype(BF16)
    lo = (r - mid.astype(F32)).astype(BF16)
    return hi, mid, lo


def _inproj_kernel(x_ref, w_ref, ws_ref, p_ref, s_ref, *, tn):
    x = x_ref[...]
    acc = jnp.dot(x, w_ref[...], preferred_element_type=F32)
    for c in range(tn // LANE):
        p_ref[c] = acc[:, c * LANE:(c + 1) * LANE]

    @pl.when(pl.program_id(1) == 0)
    def _():
        s_ref[...] = jnp.dot(x, ws_ref[...], preferred_element_type=F32)


def _wprep_kernel(a_ref, ga_ref, bd_ref, o_ref, s_ref):
    o_ref[...] = a_ref[0].T.astype(BF16)

    @pl.when(pl.program_id(2) == 0)
    def _():
        pad = jnp.zeros((LANE - GLA_RANK - 2 * DN_HEADS, ga_ref.shape[2]), F32)
        s_ref[...] = jnp.concatenate([ga_ref[0], bd_ref[0], pad], axis=0).T.astype(BF16)


def _wprep(w_t, *, tk, tn):
    depth, _, d = w_t.shape
    n_a = N_CB_A * LANE
    dn0 = n_a + GLA_RANK
    small0 = dn0 + (N_CB - N_CB_A) * LANE
    row0 = lambda j: (j * (tn // GLA_RANK) + jnp.minimum(j // (n_a // tn), 1)) * GLA_RANK
    return pl.pallas_call(
        _wprep_kernel,
        grid=(depth, d // tk, N_CB * LANE // tn),
        in_specs=[
            pl.BlockSpec((pl.Element(1), pl.Element(tn), pl.Element(tk)), lambda l, kk, j: (l, row0(j), kk * tk)),
            pl.BlockSpec((pl.Element(1), pl.Element(GLA_RANK), pl.Element(tk)), lambda l, kk, j: (l, n_a, kk * tk)),
            pl.BlockSpec((pl.Element(1), pl.Element(2 * DN_HEADS), pl.Element(tk)),
                         lambda l, kk, j: (l, small0, kk * tk)),
        ],
        out_specs=[
            pl.BlockSpec((None, tk, tn), lambda l, kk, j: (l, kk, j)),
            pl.BlockSpec((None, tk, LANE), lambda l, kk, j: (l, kk, 0)),
        ],
        out_shape=[
            jax.ShapeDtypeStruct((depth, d, N_CB * LANE), BF16),
            jax.ShapeDtypeStruct((depth, d, LANE), BF16),
        ],
        compiler_params=pltpu.CompilerParams(
            dimension_semantics=("parallel", "parallel", "arbitrary"), vmem_limit_bytes=VMEM_LIMIT),
        name="wprep",
    )(w_t, w_t, w_t)


def _inproj(xb, wcat, wsmall, layer, *, tm, tn):
    m, d = xb.shape
    n = wcat.shape[2]
    grid = (m // tm, n // tn)
    return pl.pallas_call(
        functools.partial(_inproj_kernel, tn=tn),
        grid=grid,
        in_specs=[
            pl.BlockSpec((tm, d), lambda i, j: (i, 0)),
            pl.BlockSpec((None, d, tn), lambda i, j: (layer, 0, j)),
            pl.BlockSpec((None, d, LANE), lambda i, j: (layer, 0, 0)),
        ],
        out_specs=[
            pl.BlockSpec((tn // LANE, tm, LANE), lambda i, j: (j, i, 0)),
            pl.BlockSpec((tm, LANE), lambda i, j: (i, 0)),
        ],
        out_shape=[
            jax.ShapeDtypeStruct((n // LANE, m, LANE), F32),
            jax.ShapeDtypeStruct((m, LANE), F32),
        ],
        compiler_params=pltpu.CompilerParams(
            dimension_semantics=("parallel", "arbitrary"), vmem_limit_bytes=VMEM_LIMIT),
        name="inproj",
    )(xb, wcat, wsmall)


def _ret_kernel(lg_ref, q_ref, k_ref, v_ref, g_ref, pos_ref, inv_ref, ng_ref, o_ref, s_ref, *, tb):
    @pl.when(pl.program_id(1) == 0)
    def _():
        s_ref[...] = jnp.zeros_like(s_ref)

    ang = pos_ref[...].astype(F32) * inv_ref[...]
    lane = lax.broadcasted_iota(jnp.int32, (1, LANE), 1)
    cosv = jnp.cos(ang)
    sinv = jnp.sin(ang) * jnp.where(lane < LANE // 2, -1.0, 1.0)

    def rot(t):
        return t * cosv + pltpu.roll(t, LANE // 2, 1) * sinv

    ii = lax.broadcasted_iota(jnp.int32, (tb, tb), 0)
    jj = lax.broadcasted_iota(jnp.int32, (tb, tb), 1)
    causal = ii >= jj
    dist = jnp.where(causal, (ii - jj).astype(F32), 0.0)
    ri = lax.broadcasted_iota(jnp.int32, (tb, RET_DV), 0).astype(F32)

    for h in range(RET_HEADS):
        lg = lg_ref[h]
        q = rot(q_ref[h]) * (RET_DK ** -0.5)
        k = rot(k_ref[h])
        v = jnp.concatenate([v_ref[2 * h], v_ref[2 * h + 1]], axis=1)
        dmask = jnp.where(causal, jnp.exp(dist * lg), 0.0)
        xi = jnp.exp((ri + 1.0) * lg)
        zeta = jnp.exp((tb - 1.0 - ri[:, :RET_DK]) * lg)
        g_chunk = jnp.exp(jnp.full((1, RET_DV), tb * lg, F32))

        s = s_ref[h]
        scores = _mm_nt(q, k) * dmask
        o = _mm(scores, v) + _mm(q, s) * xi
        s_ref[h] = g_chunk * s + _mm_tn(k * zeta, v)

        mu = jnp.mean(o, axis=-1, keepdims=True)
        d = o - mu
        var = jnp.mean(d * d, axis=-1, keepdims=True)
        cs = slice(h * RET_DV, (h + 1) * RET_DV)
        y = d * lax.rsqrt(var + EPS) * ng_ref[:, cs]
        gate = jnp.concatenate([g_ref[2 * h], g_ref[2 * h + 1]], axis=1)
        o_ref[:, cs] = (y * _silu(gate)).astype(o_ref.dtype)


def _retention(proj, pos_col, inv2, lg, ng, *, b, t, tb):
    m = b * t
    nt = t // tb
    row = lambda bb, tt: bb * nt + tt
    nh, nv = RET_HEADS, RET_HEADS * RET_DV // LANE
    return pl.pallas_call(
        functools.partial(_ret_kernel, tb=tb),
        grid=(b, nt),
        in_specs=[
            pl.BlockSpec(memory_space=pltpu.SMEM),
            pl.BlockSpec((nh, tb, LANE), lambda bb, tt: (CB_RQ // nh, row(bb, tt), 0)),
            pl.BlockSpec((nh, tb, LANE), lambda bb, tt: (CB_RK // nh, row(bb, tt), 0)),
            pl.BlockSpec((nv, tb, LANE), lambda bb, tt: (CB_RV // nv, row(bb, tt), 0)),
            pl.BlockSpec((nv, tb, LANE), lambda bb, tt: (CB_RG // nv, row(bb, tt), 0)),
            pl.BlockSpec((tb, 1), lambda bb, tt: (row(bb, tt), 0)),
            pl.BlockSpec((1, LANE), lambda bb, tt: (0, 0)),
            pl.BlockSpec((1, nh * RET_DV), lambda bb, tt: (0, 0)),
        ],
        out_specs=pl.BlockSpec((tb, nh * RET_DV), lambda bb, tt: (row(bb, tt), 0)),
        out_shape=jax.ShapeDtypeStruct((m, nh * RET_DV), BF16),
        scratch_shapes=[pltpu.VMEM((nh, RET_DK, RET_DV), F32)],
        compiler_params=pltpu.CompilerParams(
            dimension_semantics=("parallel", "arbitrary"), vmem_limit_bytes=VMEM_LIMIT),
        name="retention",
    )(lg, proj, proj, proj, proj, pos_col, inv2, ng)


def _gla_kernel(q_ref, k_ref, v_ref, g_ref, sm_ref, w2_ref, ba_ref, ng_ref, sel_ref,
                o_ref, st_ref, b_ref, p_ref, *, tb):
    @pl.when(pl.program_id(1) == 0)
    def _():
        st_ref[...] = jnp.zeros_like(st_ref)

    z = jnp.dot(sm_ref[...], w2_ref[...], precision=HI, preferred_element_type=F32) + ba_ref[...]
    log_a = (jnp.minimum(z, 0.0) - jnp.log1p(jnp.exp(-jnp.abs(z)))) / GLA_TAU
    ci = lax.broadcasted_iota(jnp.int32, (CHUNK, CHUNK), 0)
    cj = lax.broadcasted_iota(jnp.int32, (CHUNK, CHUNK), 1)
    nc = tb // CHUNK
    tri = jnp.broadcast_to(jnp.where(ci >= cj, 1.0, 0.0).astype(BF16)[None], (nc, CHUNK, CHUNK))
    b_all = sum(jnp.einsum("cij,cjk->cik", tri, p.reshape(nc, CHUNK, GLA_HEADS * GLA_DK),
                           preferred_element_type=F32) for p in _bf16_parts(log_a))
    for h in range(GLA_HEADS):
        b_ref[h] = b_all[:, :, h * GLA_DK:(h + 1) * GLA_DK].reshape(tb, GLA_DK)

    rid = lax.broadcasted_iota(jnp.int32, (SUB, LANE), 0)
    same_sub = (ci // SUB) == (cj // SUB)
    nsub = CHUNK // SUB
    sel = sel_ref[...]

    for c, h in [(c, h) for c in range(nc) for h in range(GLA_HEADS)]:
        r0 = c * CHUNK
        qc = q_ref[h, r0:r0 + CHUNK, :] * (GLA_DK ** -0.5)
        kc = k_ref[h, r0:r0 + CHUNK, :]
        bc = b_ref[h, r0:r0 + CHUNK, :]
        vc = jnp.concatenate([v_ref[2 * h, r0:r0 + CHUNK, :], v_ref[2 * h + 1, r0:r0 + CHUNK, :]], axis=1)

        for sb in range(nsub):
            qs = qc[sb * SUB:(sb + 1) * SUB]
            bs = bc[sb * SUB:(sb + 1) * SUB]
            for j in range(SUB):
                r = r0 + sb * SUB + j
                e = jnp.exp(jnp.where(rid >= j, bs - b_ref[h, r:r + 1, :], -jnp.inf))
                p_ref[h, sb * SUB:(sb + 1) * SUB, j * LANE:(j + 1) * LANE] = (
                    qs * k_ref[h, r:r + 1, :] * e).astype(BF16)
        diag = jnp.dot(p_ref[h], sel, preferred_element_type=F32)
        scores = jnp.where(same_sub, diag, 0.0)

        qparts, kparts = [], []
        for sb in range(1, nsub):
            lo = sb * SUB
            bref = bc[lo - 1:lo, :]
            qt = qc[lo:lo + SUB] * jnp.exp(bc[lo:lo + SUB] - bref)
            kt = kc[:lo] * jnp.exp(bref - bc[:lo])
            qparts.append(jnp.concatenate(
                [jnp.zeros((lo, LANE), F32), qt] +
                ([jnp.zeros((CHUNK - lo - SUB, LANE), F32)] if CHUNK - lo - SUB else []), axis=0))
            kparts.append(jnp.concatenate([kt, jnp.zeros((CHUNK - lo, LANE), F32)], axis=0))
        scores = scores + _mm_nt(jnp.concatenate(qparts, axis=1), jnp.concatenate(kparts, axis=1))

        st = st_ref[h]
        o = _mm(scores, vc) + _mm_nt(qc * jnp.exp(bc), st)
        b_last = bc[CHUNK - 1:CHUNK, :]
        st_ref[h] = st * jnp.exp(b_last) + _mm_tn(vc, kc * jnp.exp(b_last - bc))

        cs = slice(h * GLA_DV, (h + 1) * GLA_DV)
        y = o * lax.rsqrt(jnp.mean(o * o, axis=-1, keepdims=True) + EPS) * ng_ref[:, cs]
        gate = jnp.concatenate([g_ref[2 * h, r0:r0 + CHUNK, :], g_ref[2 * h + 1, r0:r0 + CHUNK, :]], axis=1)
        o_ref[r0:r0 + CHUNK, cs] = (y * _silu(gate)).astype(o_ref.dtype)


def _gla(proj, small, w2pad, ba, ng, sel, *, b, t, tb):
    m = b * t
    nt = t // tb
    row = lambda bb, tt: bb * nt + tt
    nh, nv = GLA_HEADS, GLA_HEADS * GLA_DV // LANE
    return pl.pallas_call(
        functools.partial(_gla_kernel, tb=tb),
        grid=(b, nt),
        in_specs=[
            pl.BlockSpec((nh, tb, LANE), lambda bb, tt: (CB_GQ // nh, row(bb, tt), 0)),
            pl.BlockSpec((nh, tb, LANE), lambda bb, tt: (CB_GK // nh, row(bb, tt), 0)),
            pl.BlockSpec((nv, tb, LANE), lambda bb, tt: (CB_GV // nv, row(bb, tt), 0)),
            pl.BlockSpec((nv, tb, LANE), lambda bb, tt: (CB_GG // nv, row(bb, tt), 0)),
            pl.BlockSpec((tb, LANE), lambda bb, tt: (row(bb, tt), 0)),
            pl.BlockSpec((LANE, nh * GLA_DK), lambda bb, tt: (0, 0)),
            pl.BlockSpec((1, nh * GLA_DK), lambda bb, tt: (0, 0)),
            pl.BlockSpec((1, nh * GLA_DV), lambda bb, tt: (0, 0)),
            pl.BlockSpec((SUB * LANE, CHUNK), lambda bb, tt: (0, 0)),
        ],
        out_specs=pl.BlockSpec((tb, nh * GLA_DV), lambda bb, tt: (row(bb, tt), 0)),
        out_shape=jax.ShapeDtypeStruct((m, nh * GLA_DV), BF16),
        scratch_shapes=[
            pltpu.VMEM((nh, GLA_DV, GLA_DK), F32),
            pltpu.VMEM((nh, tb, GLA_DK), F32),
            pltpu.VMEM((nh, CHUNK, SUB * LANE), BF16),
        ],
        compiler_params=pltpu.CompilerParams(
            dimension_semantics=("parallel", "arbitrary"), vmem_limit_bytes=VMEM_LIMIT),
        name="gla",
    )(proj, proj, proj, proj, small, w2pad, ba, ng, sel)


def _bmm(a, b):
    return jnp.einsum("cij,cjk->cik", a.astype(BF16), b.astype(BF16), preferred_element_type=F32)


def _bmm_nt(a, b):
    return jnp.einsum("cik,cjk->cij", a.astype(BF16), b.astype(BF16), preferred_element_type=F32)


def _dn_kernel(q_ref, k_ref, v_ref, g_ref, sm_ref, wq_ref, wk_ref, wv_ref, ng_ref, alog_ref, dtb_ref, pick_ref,
               o_ref, s_ref, ext_ref, wp_ref, np_ref, sb_ref, qp_ref, qu_ref, egl_ref, *, tb, hg):
    nc = tb // CHUNK

    @pl.when(pl.program_id(2) == 0)
    def _():
        s_ref[...] = jnp.zeros_like(s_ref)
        ext_ref[:, 0:8, :] = jnp.zeros((3 * hg, 8, LANE), F32)

    sm = sm_ref[...]
    lane = lax.broadcasted_iota(jnp.int32, (1, LANE), 1)
    beta_all = 1.0 / (1.0 + jnp.exp(-sm))
    g_all = -jnp.exp(alog_ref[...]) * _softplus(sm + dtb_ref[...])
    ci = lax.broadcasted_iota(jnp.int32, (CHUNK, CHUNK), 0)
    cj = lax.broadcasted_iota(jnp.int32, (CHUNK, CHUNK), 1)
    tri = jnp.broadcast_to(jnp.where(ci >= cj, 1.0, 0.0).astype(BF16)[None], (nc, CHUNK, CHUNK))
    gc_all = sum(jnp.einsum("cij,cjk->cik", tri, p.reshape(nc, CHUNK, LANE), preferred_element_type=F32)
                 for p in _bf16_parts(g_all)).reshape(tb, LANE)
    parts = jnp.concatenate(_bf16_parts(jnp.where(lane < SM_A, beta_all, gc_all)), axis=1)
    bg = jnp.dot(parts, pick_ref[...], preferred_element_type=F32)

    lower = (ci >= cj)[None]
    strict = (ci > cj)[None]
    eye = jnp.where(ci == cj, 1.0, 0.0).astype(F32)[None]

    def conv(i, raw, w):
        ext_ref[i, 8:8 + tb, :] = raw
        out = ext_ref[i, 8 - (CONV_K - 1):8 - (CONV_K - 1) + tb, :] * w[0:1, :]
        for tap in range(1, CONV_K):
            lo = 8 - (CONV_K - 1) + tap
            out = out + ext_ref[i, lo:lo + tb, :] * w[tap:tap + 1, :]
        ext_ref[i, 0:8, :] = ext_ref[i, tb:tb + 8, :]
        return _silu(out)

    for hh in range(hg):
        cs = slice(hh * LANE, (hh + 1) * LANE)
        q = conv(3 * hh, q_ref[hh], wq_ref[:, cs])
        k = conv(3 * hh + 1, k_ref[hh], wk_ref[:, cs])
        v = conv(3 * hh + 2, v_ref[hh], wv_ref[:, cs])
        q = q * (lax.rsqrt(jnp.sum(q * q, axis=-1, keepdims=True) + EPS) * (DN_DK ** -0.5))
        k = k * lax.rsqrt(jnp.sum(k * k, axis=-1, keepdims=True) + EPS)
        beta = bg[:, 2 * hh * LANE:(2 * hh + 1) * LANE]
        gc = bg[:, (2 * hh + 1) * LANE:(2 * hh + 2) * LANE]
        egc = jnp.exp(gc)

        gc3 = gc.reshape(nc, CHUNK, LANE)
        k3 = k.reshape(nc, CHUNK, LANE)
        q3 = q.reshape(nc, CHUNK, LANE)
        diff = gc3[:, :, :CHUNK] - jnp.swapaxes(gc3, 1, 2)[:, :CHUNK, :]
        decay = jnp.exp(jnp.where(lower, diff, -jnp.inf))
        a = jnp.where(strict, beta.reshape(nc, CHUNK, LANE)[:, :, :CHUNK] * _bmm_nt(k3, k3) * decay, 0.0)
        tinv = eye - a
        x = _bmm(a, a)
        for _ in range(4):
            tinv = tinv + _bmm(tinv, x)
            x = _bmm(x, x)
        tinv = tinv + _bmm(tinv, x)
        rhs = jnp.concatenate([v * beta, k * (beta * egc)], axis=1).reshape(nc, CHUNK, 2 * LANE)
        wu = _bmm(tinv, rhs).astype(BF16)
        g_last = gc3[:, CHUNK - 1:CHUNK, :]
        ke = (k3 * jnp.exp(g_last - gc3)).astype(BF16)
        qk = (_bmm_nt(q3, k3) * decay).astype(BF16)
        kt = jnp.einsum("cjk,cjd->ckd", ke, wu, preferred_element_type=F32)
        np_ref[hh] = kt[:, :, :DN_DV]
        wp_ref[hh] = kt[:, :, DN_DV:].astype(BF16)
        qt = jnp.einsum("cij,cjd->cid", qk, wu, preferred_element_type=F32)
        qu_ref[hh] = qt[:, :, :DN_DV]
        qp_ref[hh] = ((q * egc).reshape(nc, CHUNK, LANE) - qt[:, :, DN_DV:]).astype(BF16)
        egl_ref[hh] = jnp.exp(g_last)

    def chunk_step(c, carry):
        for hh in range(hg):
            s = s_ref[hh]
            sb = s.astype(BF16)
            sb_ref[hh, c] = sb
            s_ref[hh] = egl_ref[hh, c] * s + np_ref[hh, c] - jnp.dot(wp_ref[hh, c], sb, preferred_element_type=F32)
        return carry

    lax.fori_loop(0, nc, chunk_step, 0)

    for hh in range(hg):
        cs = slice(hh * LANE, (hh + 1) * LANE)
        o = (jnp.einsum("cik,ckd->cid", qp_ref[hh], sb_ref[hh], preferred_element_type=F32)
             + qu_ref[hh]).reshape(tb, LANE)
        y = o * lax.rsqrt(jnp.mean(o * o, axis=-1, keepdims=True) + EPS) * ng_ref[:, cs]
        o_ref[:, cs] = (y * _silu(g_ref[hh])).astype(o_ref.dtype)


def _deltanet(proj, small, conv_w, alog_row, dtb_row, pick, ng, *, b, t, tb, hg):
    m = b * t
    nt = t // tb
    ngrp = DN_HEADS // hg
    row = lambda bb, gg, tt: bb * nt + tt
    nc = tb // CHUNK
    return pl.pallas_call(
        functools.partial(_dn_kernel, tb=tb, hg=hg),
        grid=(b, ngrp, nt),
        in_specs=[
            pl.BlockSpec((hg, tb, LANE), lambda bb, gg, tt: (CB_DQ // hg + gg, row(bb, gg, tt), 0)),
            pl.BlockSpec((hg, tb, LANE), lambda bb, gg, tt: (CB_DK // hg + gg, row(bb, gg, tt), 0)),
            pl.BlockSpec((hg, tb, LANE), lambda bb, gg, tt: (CB_DV // hg + gg, row(bb, gg, tt), 0)),
            pl.BlockSpec((hg, tb, LANE), lambda bb, gg, tt: (CB_DG // hg + gg, row(bb, gg, tt), 0)),
            pl.BlockSpec((tb, LANE), lambda bb, gg, tt: (row(bb, gg, tt), 0)),
            pl.BlockSpec((CONV_K, hg * LANE), lambda bb, gg, tt: (0, gg)),
            pl.BlockSpec((CONV_K, hg * LANE), lambda bb, gg, tt: (0, ngrp + gg)),
            pl.BlockSpec((CONV_K, hg * LANE), lambda bb, gg, tt: (0, 2 * ngrp + gg)),
            pl.BlockSpec((1, hg * DN_DV), lambda bb, gg, tt: (0, gg)),
            pl.BlockSpec((1, LANE), lambda bb, gg, tt: (0, 0)),
            pl.BlockSpec((1, LANE), lambda bb, gg, tt: (0, 0)),
            pl.BlockSpec((None, 3 * LANE, hg * 2 * LANE), lambda bb, gg, tt: (gg, 0, 0)),
        ],
        out_specs=pl.BlockSpec((tb, hg * DN_DV), lambda bb, gg, tt: (row(bb, gg, tt), gg)),
        out_shape=jax.ShapeDtypeStruct((m, DN_HEADS * DN_DV), BF16),
        scratch_shapes=[
            pltpu.VMEM((hg, DN_DK, DN_DV), F32),
            pltpu.VMEM((3 * hg, tb + 8, LANE), F32),
            pltpu.VMEM((hg, nc, DN_DK, DN_DK), BF16),
            pltpu.VMEM((hg, nc, DN_DK, DN_DV), F32),
            pltpu.VMEM((hg, nc, DN_DK, DN_DV), BF16),
            pltpu.VMEM((hg, nc, CHUNK, DN_DK), BF16),
            pltpu.VMEM((hg, nc, CHUNK, DN_DV), F32),
            pltpu.VMEM((hg, nc, 1, LANE), F32),
        ],
        compiler_params=pltpu.CompilerParams(
            dimension_semantics=("parallel", "parallel", "arbitrary"), vmem_limit_bytes=VMEM_LIMIT),
        name="deltanet",
    )(proj, proj, proj, proj, small, conv_w, conv_w, conv_w, ng, alog_row, dtb_row, pick)


def _dn_pick_table(hg):
    ngrp = DN_HEADS // hg
    src = np.arange(3 * LANE) % LANE
    col = np.arange(hg * 2 * LANE)
    hh, is_g = col // (2 * LANE), (col // LANE) % 2
    table = np.zeros((ngrp, 3 * LANE, hg * 2 * LANE), np.float32)
    for grp in range(ngrp):
        want = np.where(is_g == 1, SM_A, SM_BETA) + grp * hg + hh
        table[grp] = src[:, None] == want[None, :]
    return table


def _outproj_kernel(ret_ref, gla_ref, dn_ref, w_ref, x_ref, g_ref, b_ref, o_ref, *maybe_ob_ref,
                    alpha, nk, k_gla, k_dn):
    kidx = pl.program_id(1)

    cols = 512

    def accumulate(mix_ref, first):
        for n0 in range(0, o_ref.shape[1], cols):
            p = jnp.dot(mix_ref[...], w_ref[:, n0:n0 + cols], preferred_element_type=F32)
            if first:
                o_ref[:, n0:n0 + cols] = p
            else:
                o_ref[:, n0:n0 + cols] += p

    pl.when(kidx == 0)(lambda: accumulate(ret_ref, True))
    pl.when((kidx > 0) & (kidx < k_gla))(lambda: accumulate(ret_ref, False))
    pl.when((kidx >= k_gla) & (kidx < k_dn))(lambda: accumulate(gla_ref, False))
    pl.when(kidx >= k_dn)(lambda: accumulate(dn_ref, False))

    @pl.when(kidx == nk - 1)
    def _():
        rows = 16
        for r0 in range(0, o_ref.shape[0], rows):
            r = alpha * x_ref[r0:r0 + rows, :] + o_ref[r0:r0 + rows, :]
            mu = jnp.mean(r, axis=-1, keepdims=True)
            d = r - mu
            var = jnp.mean(d * d, axis=-1, keepdims=True)
            y = d * lax.rsqrt(var + LN_EPS) * g_ref[...] + b_ref[...]
            o_ref[r0:r0 + rows, :] = y
            for ob_ref in maybe_ob_ref:
                ob_ref[r0:r0 + rows, :] = y.astype(BF16)


def _outproj_ln(ret, gla, dn, wout, layer, x, ln_g, ln_b, *, alpha, tm, tk, emit_bf16):
    m, d = x.shape
    k_gla = ret.shape[1] // tk
    k_dn = k_gla + gla.shape[1] // tk
    nk = k_dn + dn.shape[1] // tk
    row_block = pl.BlockSpec((tm, d), lambda i, kk: (i, 0))
    return pl.pallas_call(
        functools.partial(_outproj_kernel, alpha=alpha, nk=nk, k_gla=k_gla, k_dn=k_dn),
        grid=(m // tm, nk),
        in_specs=[
            pl.BlockSpec((tm, tk), lambda i, kk: (i, jnp.minimum(kk, k_gla - 1))),
            pl.BlockSpec((tm, tk), lambda i, kk: (i, jnp.clip(kk - k_gla, 0, k_dn - k_gla - 1))),
            pl.BlockSpec((tm, tk), lambda i, kk: (i, jnp.clip(kk - k_dn, 0, nk - k_dn - 1))),
            pl.BlockSpec((None, tk, d), lambda i, kk: (layer, kk, 0)),
            row_block,
            pl.BlockSpec((1, d), lambda i, kk: (0, 0)),
            pl.BlockSpec((1, d), lambda i, kk: (0, 0)),
        ],
        out_specs=[row_block] + [row_block] * emit_bf16,
        out_shape=[jax.ShapeDtypeStruct((m, d), F32)] + [jax.ShapeDtypeStruct((m, d), BF16)] * emit_bf16,
        compiler_params=pltpu.CompilerParams(
            dimension_semantics=("parallel", "arbitrary"), vmem_limit_bytes=VMEM_LIMIT),
        name="outproj_ln",
    )(ret, gla, dn, wout, x, ln_g, ln_b)


def _pick(n, prefs):
    for p in prefs:
        if n % p == 0:
            return p
    raise ValueError(f"no tile of {prefs} divides {n}")


def _layer(x, xb, pos_col, consts, layer, wcat, wsmall, gla_w_a2, gla_b_a, dn_conv_w, dn_a_log, dn_dt_bias,
           ret_norm_g, gla_norm_g, dn_norm_g, w_out, ln_g, ln_b, *, b, t, alpha, emit_bf16):
    inv2, lg, sel, pick = consts
    m, d = x.shape
    w2pad = jnp.concatenate(
        [gla_w_a2, jnp.zeros((LANE - GLA_RANK, GLA_HEADS * GLA_DK), gla_w_a2.dtype)], axis=0)

    proj, small = _inproj(xb, wcat, wsmall, layer, tm=_pick(m, (1024, 512, 256)), tn=_pick(N_CB * LANE, (1024,)))
    tb = _pick(t, (256,))
    ret = _retention(proj, pos_col, inv2, lg, ret_norm_g.reshape(1, -1), b=b, t=t, tb=tb)
    gla = _gla(proj, small, w2pad, gla_b_a.reshape(1, -1), gla_norm_g.reshape(1, -1), sel, b=b, t=t, tb=tb)
    lane_pad = lambda p: jnp.zeros((1, LANE), F32).at[0, SM_A:SM_A + DN_HEADS].set(p)
    dn = _deltanet(proj, small, dn_conv_w, lane_pad(dn_a_log), lane_pad(dn_dt_bias), pick,
                   dn_norm_g.reshape(1, -1), b=b, t=t, tb=_pick(t, (512, 256)), hg=DN_GROUP)
    return _outproj_ln(ret, gla, dn, w_out, layer, x, ln_g.reshape(1, -1), ln_b.reshape(1, -1),
                       alpha=alpha, tm=_pick(m, (512, 256)), tk=512, emit_bf16=emit_bf16)


def kernel(x, positions, w_in, gla_w_a2, gla_b_a, dn_conv_w, dn_a_log, dn_dt_bias, ret_norm_g, gla_norm_g,
           dn_norm_g, w_out, ln_g, ln_b):
    b, t, d = x.shape
    depth = w_in.shape[0]
    alpha = float((2 * depth) ** 0.25)
    half = RET_DK // 2
    inv = ROPE_BASE ** (-jnp.arange(half, dtype=F32) / half)
    inv2 = jnp.concatenate([inv, inv]).reshape(1, LANE)
    lg = jnp.log(1.0 - jnp.power(2.0, -5.0 - jnp.arange(RET_HEADS, dtype=F32)))
    sel = (np.arange(SUB * LANE)[:, None] // LANE == np.arange(CHUNK)[None, :] % SUB)
    consts = (inv2, lg, jnp.asarray(sel, BF16), jnp.asarray(_dn_pick_table(DN_GROUP), BF16))
    pos_col = positions.reshape(b * t, 1)
    h = x.reshape(b * t, d)
    hb = h.astype(BF16)
    wcat, wsmall = _wprep(jnp.swapaxes(w_in, 1, 2), tk=_pick(d, (2048, 1024, 512, 256)),
                          tn=_pick(N_CB_A * LANE, (1024,)))
    w_out_b = w_out.astype(BF16)
    for l in range(depth):
        h, *hb = _layer(h, hb, pos_col, consts, l, wcat, wsmall, gla_w_a2[l], gla_b_a[l], dn_conv_w[l], dn_a_log[l],
                        dn_dt_bias[l], ret_norm_g[l], gla_norm_g[l], dn_norm_g[l], w_out_b, ln_g[l], ln_b[l],
                        b=b, t=t, alpha=alpha, emit_bf16=l + 1 < depth)
        hb = hb[0] if hb else None
    return h.reshape(b, t, d)
```

```python
import functools

import jax
import jax.numpy as jnp
import numpy as np
from jax import lax
from jax.experimental import pallas as pl
from jax.experimental.pallas import tpu as pltpu

CHUNK = 64
RET_HEADS, RET_DK, RET_DV = 4, 128, 256
GLA_HEADS, GLA_DK, GLA_DV, GLA_RANK, GLA_TAU = 4, 128, 256, 16, 16.0
DN_HEADS, DN_DK, DN_DV, CONV_K = 16, 128, 128, 4
ROPE_BASE = 10000.0
EPS = 1e-6
LN_EPS = 1e-5

LANE = 128
SUB = 16
DN_GROUP = 4

CB_RQ, CB_RK, CB_RV, CB_RG = 0, 4, 8, 16
CB_GQ, CB_GK, CB_GV, CB_GG = 24, 28, 32, 40
CB_DQ, CB_DK, CB_DV, CB_DG = 48, 64, 80, 96
N_CB_A = 48
N_CB = 112
SM_GA, SM_BETA, SM_A = 0, 16, 32

VMEM_LIMIT = 60 * 1024 * 1024

F32 = jnp.float32
BF16 = jnp.bfloat16
HI = lax.Precision.HIGHEST


def _mm(a, b):
    return jnp.dot(a.astype(BF16), b.astype(BF16), preferred_element_type=F32)


def _mm_nt(a, b):
    return lax.dot_general(a.astype(BF16), b.astype(BF16), (((1,), (1,)), ((), ())),
                           preferred_element_type=F32)


def _mm_tn(a, b):
    return lax.dot_general(a.astype(BF16), b.astype(BF16), (((0,), (0,)), ((), ())),
                           preferred_element_type=F32)


def _silu(x):
    return x * (1.0 / (1.0 + jnp.exp(-x)))


def _softplus(x):
    return jnp.maximum(x, 0.0) + jnp.log1p(jnp.exp(-jnp.abs(x)))


def _bf16_parts(x):
    hi = x.astype(BF16)
    r = x - hi.astype(F32)
    mid = r.astype(BF16)
    lo = (r - mid.astype(F32)).astype(BF16)
    return hi, mid, lo


def _inproj_kernel(x_ref, w_ref, ws_ref, p_ref, s_ref, *, tn):
    x = x_ref[...]
    acc = jnp.dot(x, w_ref[...], preferred_element_type=F32)
    for c in range(tn // LANE):
        p_ref[c] = acc[:, c * LANE:(c + 1) * LANE]

    @pl.when(pl.program_id(1) == 0)
    def _():
        s_ref[...] = jnp.dot(x, ws_ref[...], preferred_element_type=F32)


def _wprep_kernel(a_ref, ga_ref, bd_ref, o_ref, s_ref):
    o_ref[...] = a_ref[0].T.astype(BF16)

    @pl.when(pl.program_id(2) == 0)
    def _():
        pad = jnp.zeros((LANE - GLA_RANK - 2 * DN_HEADS, ga_ref.shape[2]), F32)
        s_ref[...] = jnp.concatenate([ga_ref[0], bd_ref[0], pad], axis=0).T.astype(BF16)


def _wprep(w_t, *, tk, tn):
    depth, _, d = w_t.shape
    n_a = N_CB_A * LANE
    dn0 = n_a + GLA_RANK
    small0 = dn0 + (N_CB - N_CB_A) * LANE
    row0 = lambda j: (j * (tn // GLA_RANK) + jnp.minimum(j // (n_a // tn), 1)) * GLA_RANK
    return pl.pallas_call(
        _wprep_kernel,
        grid=(depth, d // tk, N_CB * LANE // tn),
        in_specs=[
            pl.BlockSpec((pl.Element(1), pl.Element(tn), pl.Element(tk)), lambda l, kk, j: (l, row0(j), kk * tk)),
            pl.BlockSpec((pl.Element(1), pl.Element(GLA_RANK), pl.Element(tk)), lambda l, kk, j: (l, n_a, kk * tk)),
            pl.BlockSpec((pl.Element(1), pl.Element(2 * DN_HEADS), pl.Element(tk)),
                         lambda l, kk, j: (l, small0, kk * tk)),
        ],
        out_specs=[
            pl.BlockSpec((None, tk, tn), lambda l, kk, j: (l, kk, j)),
            pl.BlockSpec((None, tk, LANE), lambda l, kk, j: (l, kk, 0)),
        ],
        out_shape=[
            jax.ShapeDtypeStruct((depth, d, N_CB * LANE), BF16),
            jax.ShapeDtypeStruct((depth, d, LANE), BF16),
        ],
        compiler_params=pltpu.CompilerParams(
            dimension_semantics=("parallel", "parallel", "arbitrary"), vmem_limit_bytes=VMEM_LIMIT),
        name="wprep",
    )(w_t, w_t, w_t)


def _inproj(xb, wcat, wsmall, layer, *, tm, tn):
    m, d = xb.shape
    n = wcat.shape[2]
    grid = (m // tm, n // tn)
    return pl.pallas_call(
        functools.partial(_inproj_kernel, tn=tn),
        grid=grid,
        in_specs=[
            pl.BlockSpec((tm, d), lambda i, j: (i, 0)),
            pl.BlockSpec((None, d, tn), lambda i, j: (layer, 0, j)),
            pl.BlockSpec((None, d, LANE), lambda i, j: (layer, 0, 0)),
        ],
        out_specs=[
            pl.BlockSpec((tn // LANE, tm, LANE), lambda i, j: (j, i, 0)),
            pl.BlockSpec((tm, LANE), lambda i, j: (i, 0)),
        ],
        out_shape=[
            jax.ShapeDtypeStruct((n // LANE, m, LANE), F32),
            jax.ShapeDtypeStruct((m, LANE), F32),
        ],
        compiler_params=pltpu.CompilerParams(
            dimension_semantics=("parallel", "arbitrary"), vmem_limit_bytes=VMEM_LIMIT),
        name="inproj",
    )(xb, wcat, wsmall)


def _ret_kernel(lg_ref, q_ref, k_ref, v_ref, g_ref, pos_ref, inv_ref, ng_ref, o_ref, s_ref, *, tb):
    @pl.when(pl.program_id(1) == 0)
    def _():
        s_ref[...] = jnp.zeros_like(s_ref)

    ang = pos_ref[...].astype(F32) * inv_ref[...]
    lane = lax.broadcasted_iota(jnp.int32, (1, LANE), 1)
    cosv = jnp.cos(ang)
    sinv = jnp.sin(ang) * jnp.where(lane < LANE // 2, -1.0, 1.0)

    def rot(t):
        return t * cosv + pltpu.roll(t, LANE // 2, 1) * sinv

    ii = lax.broadcasted_iota(jnp.int32, (tb, tb), 0)
    jj = lax.broadcasted_iota(jnp.int32, (tb, tb), 1)
    causal = ii >= jj
    dist = jnp.where(causal, (ii - jj).astype(F32), 0.0)
    ri = lax.broadcasted_iota(jnp.int32, (tb, RET_DV), 0).astype(F32)

    for h in range(RET_HEADS):
        lg = lg_ref[h]
        q = rot(q_ref[h]) * (RET_DK ** -0.5)
        k = rot(k_ref[h])
        v = jnp.concatenate([v_ref[2 * h], v_ref[2 * h + 1]], axis=1)
        dmask = jnp.where(causal, jnp.exp(dist * lg), 0.0)
        xi = jnp.exp((ri + 1.0) * lg)
        zeta = jnp.exp((tb - 1.0 - ri[:, :RET_DK]) * lg)
        g_chunk = jnp.exp(jnp.full((1, RET_DV), tb * lg, F32))

        s = s_ref[h]
        scores = _mm_nt(q, k) * dmask
        o = _mm(scores, v) + _mm(q, s) * xi
        s_ref[h] = g_chunk * s + _mm_tn(k * zeta, v)

        mu = jnp.mean(o, axis=-1, keepdims=True)
        d = o - mu
        var = jnp.mean(d * d, axis=-1, keepdims=True)
        cs = slice(h * RET_DV, (h + 1) * RET_DV)
        y = d * lax.rsqrt(var + EPS) * ng_ref[:, cs]
        gate = jnp.concatenate([g_ref[2 * h], g_ref[2 * h + 1]], axis=1)
        o_ref[:, cs] = (y * _silu(gate)).astype(o_ref.dtype)


def _retention(proj, pos_col, inv2, lg, ng, *, b, t, tb):
    m = b * t
    nt = t // tb
    row = lambda bb, tt: bb * nt + tt
    nh, nv = RET_HEADS, RET_HEADS * RET_DV // LANE
    return pl.pallas_call(
        functools.partial(_ret_kernel, tb=tb),
        grid=(b, nt),
        in_specs=[
            pl.BlockSpec(memory_space=pltpu.SMEM),
            pl.BlockSpec((nh, tb, LANE), lambda bb, tt: (CB_RQ // nh, row(bb, tt), 0)),
            pl.BlockSpec((nh, tb, LANE), lambda bb, tt: (CB_RK // nh, row(bb, tt), 0)),
            pl.BlockSpec((nv, tb, LANE), lambda bb, tt: (CB_RV // nv, row(bb, tt), 0)),
            pl.BlockSpec((nv, tb, LANE), lambda bb, tt: (CB_RG // nv, row(bb, tt), 0)),
            pl.BlockSpec((tb, 1), lambda bb, tt: (row(bb, tt), 0)),
            pl.BlockSpec((1, LANE), lambda bb, tt: (0, 0)),
            pl.BlockSpec((1, nh * RET_DV), lambda bb, tt: (0, 0)),
        ],
        out_specs=pl.BlockSpec((tb, nh * RET_DV), lambda bb, tt: (row(bb, tt), 0)),
        out_shape=jax.ShapeDtypeStruct((m, nh * RET_DV), BF16),
        scratch_shapes=[pltpu.VMEM((nh, RET_DK, RET_DV), F32)],
        compiler_params=pltpu.CompilerParams(
            dimension_semantics=("parallel", "arbitrary"), vmem_limit_bytes=VMEM_LIMIT),
        name="retention",
    )(lg, proj, proj, proj, proj, pos_col, inv2, ng)


def _gla_kernel(q_ref, k_ref, v_ref, g_ref, sm_ref, w2_ref, ba_ref, ng_ref, sel_ref,
                o_ref, st_ref, b_ref, p_ref, *, tb):
    @pl.when(pl.program_id(1) == 0)
    def _():
        st_ref[...] = jnp.zeros_like(st_ref)

    z = jnp.dot(sm_ref[...], w2_ref[...], precision=HI, preferred_element_type=F32) + ba_ref[...]
    log_a = (jnp.minimum(z, 0.0) - jnp.log1p(jnp.exp(-jnp.abs(z)))) / GLA_TAU
    ci = lax.broadcasted_iota(jnp.int32, (CHUNK, CHUNK), 0)
    cj = lax.broadcasted_iota(jnp.int32, (CHUNK, CHUNK), 1)
    nc = tb // CHUNK
    tri = jnp.broadcast_to(jnp.where(ci >= cj, 1.0, 0.0).astype(BF16)[None], (nc, CHUNK, CHUNK))
    b_all = sum(jnp.einsum("cij,cjk->cik", tri, p.reshape(nc, CHUNK, GLA_HEADS * GLA_DK),
                           preferred_element_type=F32) for p in _bf16_parts(log_a))
    for h in range(GLA_HEADS):
        b_ref[h] = b_all[:, :, h * GLA_DK:(h + 1) * GLA_DK].reshape(tb, GLA_DK)

    rid = lax.broadcasted_iota(jnp.int32, (SUB, LANE), 0)
    same_sub = (ci // SUB) == (cj // SUB)
    nsub = CHUNK // SUB
    sel = sel_ref[...]

    for c, h in [(c, h) for c in range(nc) for h in range(GLA_HEADS)]:
        r0 = c * CHUNK
        qc = q_ref[h, r0:r0 + CHUNK, :] * (GLA_DK ** -0.5)
        kc = k_ref[h, r0:r0 + CHUNK, :]
        bc = b_ref[h, r0:r0 + CHUNK, :]
        vc = jnp.concatenate([v_ref[2 * h, r0:r0 + CHUNK, :], v_ref[2 * h + 1, r0:r0 + CHUNK, :]], axis=1)

        for sb in range(nsub):
            qs = qc[sb * SUB:(sb + 1) * SUB]
            bs = bc[sb * SUB:(sb + 1) * SUB]
            for j in range(SUB):
                r = r0 + sb * SUB + j
                e = jnp.exp(jnp.where(rid >= j, bs - b_ref[h, r:r + 1, :], -jnp.inf))
                p_ref[h, sb * SUB:(sb + 1) * SUB, j * LANE:(j + 1) * LANE] = (
                    qs * k_ref[h, r:r + 1, :] * e).astype(BF16)
        diag = jnp.dot(p_ref[h], sel, preferred_element_type=F32)
        scores = jnp.where(same_sub, diag, 0.0)

        qparts, kparts = [], []
        for sb in range(1, nsub):
            lo = sb * SUB
            bref = bc[lo - 1:lo, :]
            qt = qc[lo:lo + SUB] * jnp.exp(bc[lo:lo + SUB] - bref)
            kt = kc[:lo] * jnp.exp(bref - bc[:lo])
            qparts.append(jnp.concatenate(
                [jnp.zeros((lo, LANE), F32), qt] +
                ([jnp.zeros((CHUNK - lo - SUB, LANE), F32)] if CHUNK - lo - SUB else []), axis=0))
            kparts.append(jnp.concatenate([kt, jnp.zeros((CHUNK - lo, LANE), F32)], axis=0))
        scores = scores + _mm_nt(jnp.concatenate(qparts, axis=1), jnp.concatenate(kparts, axis=1))

        st = st_ref[h]
        o = _mm(scores, vc) + _mm_nt(qc * jnp.exp(bc), st)
        b_last = bc[CHUNK - 1:CHUNK, :]
        st_ref[h] = st * jnp.exp(b_last) + _mm_tn(vc, kc * jnp.exp(b_last - bc))

        cs = slice(h * GLA_DV, (h + 1) * GLA_DV)
        y = o * lax.rsqrt(jnp.mean(o * o, axis=-1, keepdims=True) + EPS) * ng_ref[:, cs]
        gate = jnp.concatenate([g_ref[2 * h, r0:r0 + CHUNK, :], g_ref[2 * h + 1, r0:r0 + CHUNK, :]], axis=1)
        o_ref[r0:r0 + CHUNK, cs] = (y * _silu(gate)).astype(o_ref.dtype)


def _gla(proj, small, w2pad, ba, ng, sel, *, b, t, tb):
    m = b * t
    nt = t // tb
    row = lambda bb, tt: bb * nt + tt
    nh, nv = GLA_HEADS, GLA_HEADS * GLA_DV // LANE
    return pl.pallas_call(
        functools.partial(_gla_kernel, tb=tb),
        grid=(b, nt),
        in_specs=[
            pl.BlockSpec((nh, tb, LANE), lambda bb, tt: (CB_GQ // nh, row(bb, tt), 0)),
            pl.BlockSpec((nh, tb, LANE), lambda bb, tt: (CB_GK // nh, row(bb, tt), 0)),
            pl.BlockSpec((nv, tb, LANE), lambda bb, tt: (CB_GV // nv, row(bb, tt), 0)),
            pl.BlockSpec((nv, tb, LANE), lambda bb, tt: (CB_GG // nv, row(bb, tt), 0)),
            pl.BlockSpec((tb, LANE), lambda bb, tt: (row(bb, tt), 0)),
            pl.BlockSpec((LANE, nh * GLA_DK), lambda bb, tt: (0, 0)),
            pl.BlockSpec((1, nh * GLA_DK), lambda bb, tt: (0, 0)),
            pl.BlockSpec((1, nh * GLA_DV), lambda bb, tt: (0, 0)),
            pl.BlockSpec((SUB * LANE, CHUNK), lambda bb, tt: (0, 0)),
        ],
        out_specs=pl.BlockSpec((tb, nh * GLA_DV), lambda bb, tt: (row(bb, tt), 0)),
        out_shape=jax.ShapeDtypeStruct((m, nh * GLA_DV), BF16),
        scratch_shapes=[
            pltpu.VMEM((nh, GLA_DV, GLA_DK), F32),
            pltpu.VMEM((nh, tb, GLA_DK), F32),
            pltpu.VMEM((nh, CHUNK, SUB * LANE), BF16),
        ],
        compiler_params=pltpu.CompilerParams(
            dimension_semantics=("parallel", "arbitrary"), vmem_limit_bytes=VMEM_LIMIT),
        name="gla",
    )(proj, proj, proj, proj, small, w2pad, ba, ng, sel)


def _bmm(a, b):
    return jnp.einsum("cij,cjk->cik", a.astype(BF16), b.astype(BF16), preferred_element_type=F32)


def _bmm_nt(a, b):
    return jnp.einsum("cik,cjk->cij", a.astype(BF16), b.astype(BF16), preferred_element_type=F32)


def _dn_kernel(q_ref, k_ref, v_ref, g_ref, sm_ref, wq_ref, wk_ref, wv_ref, ng_ref, alog_ref, dtb_ref, pick_ref,
               o_ref, s_ref, ext_ref, wp_ref, np_ref, sb_ref, qp_ref, qu_ref, egl_ref, *, tb, hg):
    nc = tb // CHUNK

    @pl.when(pl.program_id(2) == 0)
    def _():
        s_ref[...] = jnp.zeros_like(s_ref)
        ext_ref[:, 0:8, :] = jnp.zeros((3 * hg, 8, LANE), F32)

    sm = sm_ref[...]
    lane = lax.broadcasted_iota(jnp.int32, (1, LANE), 1)
    beta_all = 1.0 / (1.0 + jnp.exp(-sm))
    g_all = -jnp.exp(alog_ref[...]) * _softplus(sm + dtb_ref[...])
    ci = lax.broadcasted_iota(jnp.int32, (CHUNK, CHUNK), 0)
    cj = lax.broadcasted_iota(jnp.int32, (CHUNK, CHUNK), 1)
    tri = jnp.broadcast_to(jnp.where(ci >= cj, 1.0, 0.0).astype(BF16)[None], (nc, CHUNK, CHUNK))
    gc_all = sum(jnp.einsum("cij,cjk->cik", tri, p.reshape(nc, CHUNK, LANE), preferred_element_type=F32)
                 for p in _bf16_parts(g_all)).reshape(tb, LANE)
    parts = jnp.concatenate(_bf16_parts(jnp.where(lane < SM_A, beta_all, gc_all)), axis=1)
    bg = jnp.dot(parts, pick_ref[...], preferred_element_type=F32)

    lower = (ci >= cj)[None]
    strict = (ci > cj)[None]
    eye = jnp.where(ci == cj, 1.0, 0.0).astype(F32)[None]

    def conv(i, raw, w):
        ext_ref[i, 8:8 + tb, :] = raw
        out = ext_ref[i, 8 - (CONV_K - 1):8 - (CONV_K - 1) + tb, :] * w[0:1, :]
        for tap in range(1, CONV_K):
            lo = 8 - (CONV_K - 1) + tap
            out = out + ext_ref[i, lo:lo + tb, :] * w[tap:tap + 1, :]
        ext_ref[i, 0:8, :] = ext_ref[i, tb:tb + 8, :]
        return _silu(out)

    for hh in range(hg):
        cs = slice(hh * LANE, (hh + 1) * LANE)
        q = conv(3 * hh, q_ref[hh], wq_ref[:, cs])
        k = conv(3 * hh + 1, k_ref[hh], wk_ref[:, cs])
        v = conv(3 * hh + 2, v_ref[hh], wv_ref[:, cs])
        q = q * (lax.rsqrt(jnp.sum(q * q, axis=-1, keepdims=True) + EPS) * (DN_DK ** -0.5))
        k = k * lax.rsqrt(jnp.sum(k * k, axis=-1, keepdims=True) + EPS)
        beta = bg[:, 2 * hh * LANE:(2 * hh + 1) * LANE]
        gc = bg[:, (2 * hh + 1) * LANE:(2 * hh + 2) * LANE]
        egc = jnp.exp(gc)

        gc3 = gc.reshape(nc, CHUNK, LANE)
        k3 = k.reshape(nc, CHUNK, LANE)
        q3 = q.reshape(nc, CHUNK, LANE)
        diff = gc3[:, :, :CHUNK] - jnp.swapaxes(gc3, 1, 2)[:, :CHUNK, :]
        decay = jnp.exp(jnp.where(lower, diff, -jnp.inf))
        a = jnp.where(strict, beta.reshape(nc, CHUNK, LANE)[:, :, :CHUNK] * _bmm_nt(k3, k3) * decay, 0.0)
        tinv = eye - a
        x = _bmm(a, a)
        for _ in range(4):
            tinv = tinv + _bmm(tinv, x)
            x = _bmm(x, x)
        tinv = tinv + _bmm(tinv, x)
        rhs = jnp.concatenate([v * beta, k * (beta * egc)], axis=1).reshape(nc, CHUNK, 2 * LANE)
        wu = _bmm(tinv, rhs).astype(BF16)
        g_last = gc3[:, CHUNK - 1:CHUNK, :]
        ke = (k3 * jnp.exp(g_last - gc3)).astype(BF16)
        qk = (_bmm_nt(q3, k3) * decay).astype(BF16)
        kt = jnp.einsum("cjk,cjd->ckd", ke, wu, preferred_element_type=F32)
        np_ref[hh] = kt[:, :, :DN_DV]
        wp_ref[hh] = kt[:, :, DN_DV:].astype(BF16)
        qt = jnp.einsum("cij,cjd->cid", qk, wu, preferred_element_type=F32)
        qu_ref[hh] = qt[:, :, :DN_DV]
        qp_ref[hh] = ((q * egc).reshape(nc, CHUNK, LANE) - qt[:, :, DN_DV:]).astype(BF16)
        egl_ref[hh] = jnp.exp(g_last)

    def chunk_step(c, carry):
        for hh in range(hg):
            s = s_ref[hh]
            sb = s.astype(BF16)
            sb_ref[hh, c] = sb
            s_ref[hh] = egl_ref[hh, c] * s + np_ref[hh, c] - jnp.dot(wp_ref[hh, c], sb, preferred_element_type=F32)
        return carry

    lax.fori_loop(0, nc, chunk_step, 0)

    for hh in range(hg):
        cs = slice(hh * LANE, (hh + 1) * LANE)
        o = (jnp.einsum("cik,ckd->cid", qp_ref[hh], sb_ref[hh], preferred_element_type=F32)
             + qu_ref[hh]).reshape(tb, LANE)
        y = o * lax.rsqrt(jnp.mean(o * o, axis=-1, keepdims=True) + EPS) * ng_ref[:, cs]
        o_ref[:, cs] = (y * _silu(g_ref[hh])).astype(o_ref.dtype)


def _deltanet(proj, small, conv_w, alog_row, dtb_row, pick, ng, *, b, t, tb, hg):
    m = b * t
    nt = t // tb
    ngrp = DN_HEADS // hg
    row = lambda bb, gg, tt: bb * nt + tt
    nc = tb // CHUNK
    return pl.pallas_call(
        functools.partial(_dn_kernel, tb=tb, hg=hg),
        grid=(b, ngrp, nt),
        in_specs=[
            pl.BlockSpec((hg, tb, LANE), lambda bb, gg, tt: (CB_DQ // hg + gg, row(bb, gg, tt), 0)),
            pl.BlockSpec((hg, tb, LANE), lambda bb, gg, tt: (CB_DK // hg + gg, row(bb, gg, tt), 0)),
            pl.BlockSpec((hg, tb, LANE), lambda bb, gg, tt: (CB_DV // hg + gg, row(bb, gg, tt), 0)),
            pl.BlockSpec((hg, tb, LANE), lambda bb, gg, tt: (CB_DG // hg + gg, row(bb, gg, tt), 0)),
            pl.BlockSpec((tb, LANE), lambda bb, gg, tt: (row(bb, gg, tt), 0)),
            pl.BlockSpec((CONV_K, hg * LANE), lambda bb, gg, tt: (0, gg)),
            pl.BlockSpec((CONV_K, hg * LANE), lambda bb, gg, tt: (0, ngrp + gg)),
            pl.BlockSpec((CONV_K, hg * LANE), lambda bb, gg, tt: (0, 2 * ngrp + gg)),
            pl.BlockSpec((1, hg * DN_DV), lambda bb, gg, tt: (0, gg)),
            pl.BlockSpec((1, LANE), lambda bb, gg, tt: (0, 0)),
            pl.BlockSpec((1, LANE), lambda bb, gg, tt: (0, 0)),
            pl.BlockSpec((None, 3 * LANE, hg * 2 * LANE), lambda bb, gg, tt: (gg, 0, 0)),
        ],
        out_specs=pl.BlockSpec((tb, hg * DN_DV), lambda bb, gg, tt: (row(bb, gg, tt), gg)),
        out_shape=jax.ShapeDtypeStruct((m, DN_HEADS * DN_DV), BF16),
        scratch_shapes=[
            pltpu.VMEM((hg, DN_DK, DN_DV), F32),
            pltpu.VMEM((3 * hg, tb + 8, LANE), F32),
            pltpu.VMEM((hg, nc, DN_DK, DN_DK), BF16),
            pltpu.VMEM((hg, nc, DN_DK, DN_DV), F32),
            pltpu.VMEM((hg, nc, DN_DK, DN_DV), BF16),
            pltpu.VMEM((hg, nc, CHUNK, DN_DK), BF16),
            pltpu.VMEM((hg, nc, CHUNK, DN_DV), F32),
            pltpu.VMEM((hg, nc, 1, LANE), F32),
        ],
        compiler_params=pltpu.CompilerParams(
            dimension_semantics=("parallel", "parallel", "arbitrary"), vmem_limit_bytes=VMEM_LIMIT),
        name="deltanet",
    )(proj, proj, proj, proj, small, conv_w, conv_w, conv_w, ng, alog_row, dtb_row, pick)


def _dn_pick_table(hg):
    ngrp = DN_HEADS // hg
    src = np.arange(3 * LANE) % LANE
    col = np.arange(hg * 2 * LANE)
    hh, is_g = col // (2 * LANE), (col // LANE) % 2
    table = np.zeros((ngrp, 3 * LANE, hg * 2 * LANE), np.float32)
    for grp in range(ngrp):
        want = np.where(is_g == 1, SM_A, SM_BETA) + grp * hg + hh
        table[grp] = src[:, None] == want[None, :]
    return table


def _outproj_kernel(ret_ref, gla_ref, dn_ref, w_ref, x_ref, g_ref, b_ref, o_ref, *maybe_ob_ref,
                    alpha, nk, k_gla, k_dn):
    kidx = pl.program_id(1)

    cols = 512

    def accumulate(mix_ref, first):
        for n0 in range(0, o_ref.shape[1], cols):
            p = jnp.dot(mix_ref[...], w_ref[:, n0:n0 + cols], preferred_element_type=F32)
            if first:
                o_ref[:, n0:n0 + cols] = p
            else:
                o_ref[:, n0:n0 + cols] += p

    pl.when(kidx == 0)(lambda: accumulate(ret_ref, True))
    pl.when((kidx > 0) & (kidx < k_gla))(lambda: accumulate(ret_ref, False))
    pl.when((kidx >= k_gla) & (kidx < k_dn))(lambda: accumulate(gla_ref, False))
    pl.when(kidx >= k_dn)(lambda: accumulate(dn_ref, False))

    @pl.when(kidx == nk - 1)
    def _():
        rows = 16
        for r0 in range(0, o_ref.shape[0], rows):
            r = alpha * x_ref[r0:r0 + rows, :] + o_ref[r0:r0 + rows, :]
            mu = jnp.mean(r, axis=-1, keepdims=True)
            d = r - mu
            var = jnp.mean(d * d, axis=-1, keepdims=True)
            y = d * lax.rsqrt(var + LN_EPS) * g_ref[...] + b_ref[...]
            o_ref[r0:r0 + rows, :] = y
            for ob_ref in maybe_ob_ref:
                ob_ref[r0:r0 + rows, :] = y.astype(BF16)


def _outproj_ln(ret, gla, dn, wout, layer, x, ln_g, ln_b, *, alpha, tm, tk, emit_bf16):
    m, d = x.shape
    k_gla = ret.shape[1] // tk
    k_dn = k_gla + gla.shape[1] // tk
    nk = k_dn + dn.shape[1] // tk
    row_block = pl.BlockSpec((tm, d), lambda i, kk: (i, 0))
    return pl.pallas_call(
        functools.partial(_outproj_kernel, alpha=alpha, nk=nk, k_gla=k_gla, k_dn=k_dn),
        grid=(m // tm, nk),
        in_specs=[
            pl.BlockSpec((tm, tk), lambda i, kk: (i, jnp.minimum(kk, k_gla - 1))),
            pl.BlockSpec((tm, tk), lambda i, kk: (i, jnp.clip(kk - k_gla, 0, k_dn - k_gla - 1))),
            pl.BlockSpec((tm, tk), lambda i, kk: (i, jnp.clip(kk - k_dn, 0, nk - k_dn - 1))),
            pl.BlockSpec((None, tk, d), lambda i, kk: (layer, kk, 0)),
            row_block,
            pl.BlockSpec((1, d), lambda i, kk: (0, 0)),
            pl.BlockSpec((1, d), lambda i, kk: (0, 0)),
        ],
        out_specs=[row_block] + [row_block] * emit_bf16,
        out_shape=[jax.ShapeDtypeStruct((m, d), F32)] + [jax.ShapeDtypeStruct((m, d), BF16)] * emit_bf16,
        compiler_params=pltpu.CompilerParams(
            dimension_semantics=("parallel", "arbitrary"), vmem_limit_bytes=VMEM_LIMIT),
        name="outproj_ln",
    )(ret, gla, dn, wout, x, ln_g, ln_b)


def _pick(n, prefs):
    for p in prefs:
        if n % p == 0:
            return p
    raise ValueError(f"no tile of {prefs} divides {n}")


def _layer(x, xb, pos_col, consts, layer, wcat, wsmall, gla_w_a2, gla_b_a, dn_conv_w, dn_a_log, dn_dt_bias,
           ret_norm_g, gla_norm_g, dn_norm_g, w_out, ln_g, ln_b, *, b, t, alpha, emit_bf16):
    inv2, lg, sel, pick = consts
    m, d = x.shape
    w2pad = jnp.concatenate(
        [gla_w_a2, jnp.zeros((LANE - GLA_RANK, GLA_HEADS * GLA_DK), gla_w_a2.dtype)], axis=0)

    proj, small = _inproj(xb, wcat, wsmall, layer, tm=_pick(m, (1024, 512, 256)), tn=_pick(N_CB * LANE, (1024,)))
    ret = _retention(proj, pos_col, inv2, lg, ret_norm_g.reshape(1, -1), b=b, t=t, tb=_pick(t, (256,)))
    gla = _gla(proj, small, w2pad, gla_b_a.reshape(1, -1), gla_norm_g.reshape(1, -1), sel,
               b=b, t=t, tb=_pick(t, (512, 256)))
    lane_pad = lambda p: jnp.zeros((1, LANE), F32).at[0, SM_A:SM_A + DN_HEADS].set(p)
    dn = _deltanet(proj, small, dn_conv_w, lane_pad(dn_a_log), lane_pad(dn_dt_bias), pick,
                   dn_norm_g.reshape(1, -1), b=b, t=t, tb=_pick(t, (1024, 512, 256)), hg=DN_GROUP)
    return _outproj_ln(ret, gla, dn, w_out, layer, x, ln_g.reshape(1, -1), ln_b.reshape(1, -1),
                       alpha=alpha, tm=_pick(m, (512, 256)), tk=512, emit_bf16=emit_bf16)


def kernel(x, positions, w_in, gla_w_a2, gla_b_a, dn_conv_w, dn_a_log, dn_dt_bias, ret_norm_g, gla_norm_g,
           dn_norm_g, w_out, ln_g, ln_b):
    b, t, d = x.shape
    depth = w_in.shape[0]
    alpha = float((2 * depth) ** 0.25)
    half = RET_DK // 2
    inv = ROPE_BASE ** (-jnp.arange(half, dtype=F32) / half)
    inv2 = jnp.concatenate([inv, inv]).reshape(1, LANE)
    lg = jnp.log(1.0 - jnp.power(2.0, -5.0 - jnp.arange(RET_HEADS, dtype=F32)))
    sel = (np.arange(SUB * LANE)[:, None] // LANE == np.arange(CHUNK)[None, :] % SUB)
    consts = (inv2, lg, jnp.asarray(sel, BF16), jnp.asarray(_dn_pick_table(DN_GROUP), BF16))
    pos_col = positions.reshape(b * t, 1)
    h = x.reshape(b * t, d)
    hb = h.astype(BF16)
    wcat, wsmall = _wprep(jnp.swapaxes(w_in, 1, 2), tk=_pick(d, (2048, 1024, 512, 256)),
                          tn=_pick(N_CB_A * LANE, (1024,)))
    w_out_b = w_out.astype(BF16)
    for l in range(depth):
        h, *hb = _layer(h, hb, pos_col, consts, l, wcat, wsmall, gla_w_a2[l], gla_b_a[l], dn_conv_w[l], dn_a_log[l],
                        dn_dt_bias[l], ret_norm_g[l], gla_norm_g[l], dn_norm_g[l], w_out_b, ln_g[l], ln_b[l],
                        b=b, t=t, alpha=alpha, emit_bf16=l + 1 < depth)
        hb = hb[0] if hb else None
    return h.reshape(b, t, d)
```

```python
import functools

import jax
import jax.numpy as jnp
import numpy as np
from jax import lax
from jax.experimental import pallas as pl
from jax.experimental.pallas import tpu as pltpu

CHUNK = 64
RET_HEADS, RET_DK, RET_DV = 4, 128, 256
GLA_HEADS, GLA_DK, GLA_DV, GLA_RANK, GLA_TAU = 4, 128, 256, 16, 16.0
DN_HEADS, DN_DK, DN_DV, CONV_K = 16, 128, 128, 4
ROPE_BASE = 10000.0
EPS = 1e-6
LN_EPS = 1e-5

LANE = 128
SUB = 16
DN_GROUP = 4

CB_RQ, CB_RK, CB_RV, CB_RG = 0, 4, 8, 16
CB_GQ, CB_GK, CB_GV, CB_GG = 24, 28, 32, 40
CB_DQ, CB_DK, CB_DV, CB_DG = 48, 64, 80, 96
N_CB_A = 48
N_CB = 112
SM_BETA, SM_A = 16, 32

VMEM_LIMIT = 60 * 1024 * 1024

F32 = jnp.float32
BF16 = jnp.bfloat16
HI = lax.Precision.HIGHEST


def _mm(a, b):
    return jnp.dot(a.astype(BF16), b.astype(BF16), preferred_element_type=F32)


def _mm_nt(a, b):
    return lax.dot_general(a.astype(BF16), b.astype(BF16), (((1,), (1,)), ((), ())),
                           preferred_element_type=F32)


def _mm_tn(a, b):
    return lax.dot_general(a.astype(BF16), b.astype(BF16), (((0,), (0,)), ((), ())),
                           preferred_element_type=F32)


def _silu(x):
    return x * (1.0 / (1.0 + jnp.exp(-x)))


def _softplus(x):
    return jnp.maximum(x, 0.0) + jnp.log1p(jnp.exp(-jnp.abs(x)))


def _bf16_parts(x):
    hi = x.astype(BF16)
    r = x - hi.astype(F32)
    mid = r.astype(BF16)
    lo = (r - mid.astype(F32)).astype(BF16)
    return hi, mid, lo


def _inproj_kernel(x_ref, w_ref, ws_ref, p_ref, s_ref, *, tn):
    x = x_ref[...]
    acc = jnp.dot(x, w_ref[...], preferred_element_type=F32)
    for c in range(tn // LANE):
        p_ref[c] = acc[:, c * LANE:(c + 1) * LANE]

    @pl.when(pl.program_id(1) == 0)
    def _():
        s_ref[...] = jnp.dot(x, ws_ref[...], preferred_element_type=F32)


def _wprep_kernel(a_ref, ga_ref, bd_ref, o_ref, s_ref):
    o_ref[...] = a_ref[0].T.astype(BF16)

    @pl.when(pl.program_id(2) == 0)
    def _():
        pad = jnp.zeros((LANE - GLA_RANK - 2 * DN_HEADS, ga_ref.shape[2]), F32)
        s_ref[...] = jnp.concatenate([ga_ref[0], bd_ref[0], pad], axis=0).T.astype(BF16)


def _wprep(w_t, *, tk, tn):
    depth, _, d = w_t.shape
    n_a = N_CB_A * LANE
    dn0 = n_a + GLA_RANK
    small0 = dn0 + (N_CB - N_CB_A) * LANE
    row0 = lambda j: (j * (tn // GLA_RANK) + jnp.minimum(j // (n_a // tn), 1)) * GLA_RANK
    return pl.pallas_call(
        _wprep_kernel,
        grid=(depth, d // tk, N_CB * LANE // tn),
        in_specs=[
            pl.BlockSpec((pl.Element(1), pl.Element(tn), pl.Element(tk)), lambda l, kk, j: (l, row0(j), kk * tk)),
            pl.BlockSpec((pl.Element(1), pl.Element(GLA_RANK), pl.Element(tk)), lambda l, kk, j: (l, n_a, kk * tk)),
            pl.BlockSpec((pl.Element(1), pl.Element(2 * DN_HEADS), pl.Element(tk)),
                         lambda l, kk, j: (l, small0, kk * tk)),
        ],
        out_specs=[
            pl.BlockSpec((None, tk, tn), lambda l, kk, j: (l, kk, j)),
            pl.BlockSpec((None, tk, LANE), lambda l, kk, j: (l, kk, 0)),
        ],
        out_shape=[
            jax.ShapeDtypeStruct((depth, d, N_CB * LANE), BF16),
            jax.ShapeDtypeStruct((depth, d, LANE), BF16),
        ],
        compiler_params=pltpu.CompilerParams(
            dimension_semantics=("parallel", "parallel", "arbitrary"), vmem_limit_bytes=VMEM_LIMIT),
        name="wprep",
    )(w_t, w_t, w_t)


def _inproj(xb, wcat, wsmall, layer, *, tm, tn):
    m, d = xb.shape
    n = wcat.shape[2]
    grid = (m // tm, n // tn)
    return pl.pallas_call(
        functools.partial(_inproj_kernel, tn=tn),
        grid=grid,
        in_specs=[
            pl.BlockSpec((tm, d), lambda i, j: (i, 0)),
            pl.BlockSpec((None, d, tn), lambda i, j: (layer, 0, j)),
            pl.BlockSpec((None, d, LANE), lambda i, j: (layer, 0, 0)),
        ],
        out_specs=[
            pl.BlockSpec((tn // LANE, tm, LANE), lambda i, j: (j, i, 0)),
            pl.BlockSpec((tm, LANE), lambda i, j: (i, 0)),
        ],
        out_shape=[
            jax.ShapeDtypeStruct((n // LANE, m, LANE), F32),
            jax.ShapeDtypeStruct((m, LANE), F32),
        ],
        compiler_params=pltpu.CompilerParams(
            dimension_semantics=("parallel", "arbitrary"), vmem_limit_bytes=VMEM_LIMIT),
        name="inproj",
    )(xb, wcat, wsmall)


def _gates_kernel(sm_ref, w2_ref, ba_ref, alog_ref, dtb_ref, b_ref, parts_ref, *, tb):
    nc = tb // CHUNK
    ci = lax.broadcasted_iota(jnp.int32, (CHUNK, CHUNK), 0)
    cj = lax.broadcasted_iota(jnp.int32, (CHUNK, CHUNK), 1)
    tri = jnp.broadcast_to(jnp.where(ci >= cj, 1.0, 0.0).astype(BF16)[None], (nc, CHUNK, CHUNK))

    def chunk_cumsum(v):
        n = v.shape[1]
        return sum(jnp.einsum("cij,cjk->cik", tri, p.reshape(nc, CHUNK, n), preferred_element_type=F32)
                   for p in _bf16_parts(v)).reshape(tb, n)

    sm = sm_ref[...]
    z = jnp.dot(sm, w2_ref[...], precision=HI, preferred_element_type=F32) + ba_ref[...]
    b_ref[...] = chunk_cumsum((jnp.minimum(z, 0.0) - jnp.log1p(jnp.exp(-jnp.abs(z)))) / GLA_TAU)
    lane = lax.broadcasted_iota(jnp.int32, (1, LANE), 1)
    beta_all = 1.0 / (1.0 + jnp.exp(-sm))
    gc_all = chunk_cumsum(-jnp.exp(alog_ref[...]) * _softplus(sm + dtb_ref[...]))
    parts_ref[...] = jnp.concatenate(_bf16_parts(jnp.where(lane < SM_A, beta_all, gc_all)), axis=1)


def _gates(small, w2pad, ba, alog_row, dtb_row, *, tb):
    m = small.shape[0]
    n_gla = w2pad.shape[1]
    whole = lambda shape: pl.BlockSpec(shape, lambda i: (0, 0))
    return pl.pallas_call(
        functools.partial(_gates_kernel, tb=tb),
        grid=(m // tb,),
        in_specs=[
            pl.BlockSpec((tb, LANE), lambda i: (i, 0)),
            whole((LANE, n_gla)), whole((1, n_gla)), whole((1, LANE)), whole((1, LANE)),
        ],
        out_specs=[
            pl.BlockSpec((tb, n_gla), lambda i: (i, 0)),
            pl.BlockSpec((tb, 3 * LANE), lambda i: (i, 0)),
        ],
        out_shape=[
            jax.ShapeDtypeStruct((m, n_gla), F32),
            jax.ShapeDtypeStruct((m, 3 * LANE), BF16),
        ],
        compiler_params=pltpu.CompilerParams(
            dimension_semantics=("parallel",), vmem_limit_bytes=VMEM_LIMIT),
        name="gates",
    )(small, w2pad, ba, alog_row, dtb_row)


def _ret_kernel(lg_ref, q_ref, k_ref, v_ref, g_ref, pos_ref, inv_ref, ng_ref, o_ref, s_ref, *, tb):
    @pl.when(pl.program_id(1) == 0)
    def _():
        s_ref[...] = jnp.zeros_like(s_ref)

    ang = pos_ref[...].astype(F32) * inv_ref[...]
    lane = lax.broadcasted_iota(jnp.int32, (1, LANE), 1)
    cosv = jnp.cos(ang)
    sinv = jnp.sin(ang) * jnp.where(lane < LANE // 2, -1.0, 1.0)

    def rot(t):
        return t * cosv + pltpu.roll(t, LANE // 2, 1) * sinv

    ii = lax.broadcasted_iota(jnp.int32, (tb, tb), 0)
    jj = lax.broadcasted_iota(jnp.int32, (tb, tb), 1)
    causal = ii >= jj
    dist = jnp.where(causal, (ii - jj).astype(F32), 0.0)
    ri = lax.broadcasted_iota(jnp.int32, (tb, RET_DV), 0).astype(F32)

    for h in range(RET_HEADS):
        lg = lg_ref[h]
        q = rot(q_ref[h]) * (RET_DK ** -0.5)
        k = rot(k_ref[h])
        v = jnp.concatenate([v_ref[2 * h], v_ref[2 * h + 1]], axis=1)
        dmask = jnp.where(causal, jnp.exp(dist * lg), 0.0)
        xi = jnp.exp((ri + 1.0) * lg)
        zeta = jnp.exp((tb - 1.0 - ri[:, :RET_DK]) * lg)
        g_chunk = jnp.exp(jnp.full((1, RET_DV), tb * lg, F32))

        s = s_ref[h]
        scores = _mm_nt(q, k) * dmask
        o = _mm(scores, v) + _mm(q, s) * xi
        s_ref[h] = g_chunk * s + _mm_tn(k * zeta, v)

        mu = jnp.mean(o, axis=-1, keepdims=True)
        d = o - mu
        var = jnp.mean(d * d, axis=-1, keepdims=True)
        cs = slice(h * RET_DV, (h + 1) * RET_DV)
        y = d * lax.rsqrt(var + EPS) * ng_ref[:, cs]
        gate = jnp.concatenate([g_ref[2 * h], g_ref[2 * h + 1]], axis=1)
        o_ref[:, cs] = (y * _silu(gate)).astype(o_ref.dtype)


def _retention(proj, pos_col, inv2, lg, ng, *, b, t, tb):
    m = b * t
    nt = t // tb
    row = lambda bb, tt: bb * nt + tt
    nh, nv = RET_HEADS, RET_HEADS * RET_DV // LANE
    return pl.pallas_call(
        functools.partial(_ret_kernel, tb=tb),
        grid=(b, nt),
        in_specs=[
            pl.BlockSpec(memory_space=pltpu.SMEM),
            pl.BlockSpec((nh, tb, LANE), lambda bb, tt: (CB_RQ // nh, row(bb, tt), 0)),
            pl.BlockSpec((nh, tb, LANE), lambda bb, tt: (CB_RK // nh, row(bb, tt), 0)),
            pl.BlockSpec((nv, tb, LANE), lambda bb, tt: (CB_RV // nv, row(bb, tt), 0)),
            pl.BlockSpec((nv, tb, LANE), lambda bb, tt: (CB_RG // nv, row(bb, tt), 0)),
            pl.BlockSpec((tb, 1), lambda bb, tt: (row(bb, tt), 0)),
            pl.BlockSpec((1, LANE), lambda bb, tt: (0, 0)),
            pl.BlockSpec((1, nh * RET_DV), lambda bb, tt: (0, 0)),
        ],
        out_specs=pl.BlockSpec((tb, nh * RET_DV), lambda bb, tt: (row(bb, tt), 0)),
        out_shape=jax.ShapeDtypeStruct((m, nh * RET_DV), BF16),
        scratch_shapes=[pltpu.VMEM((nh, RET_DK, RET_DV), F32)],
        compiler_params=pltpu.CompilerParams(
            dimension_semantics=("parallel", "arbitrary"), vmem_limit_bytes=VMEM_LIMIT),
        name="retention",
    )(lg, proj, proj, proj, proj, pos_col, inv2, ng)


def _gla_kernel(q_ref, k_ref, v_ref, g_ref, b_ref, ng_ref, sel_ref, o_ref, st_ref, p_ref, *, tb):
    @pl.when(pl.program_id(1) == 0)
    def _():
        st_ref[...] = jnp.zeros_like(st_ref)

    ci = lax.broadcasted_iota(jnp.int32, (CHUNK, CHUNK), 0)
    cj = lax.broadcasted_iota(jnp.int32, (CHUNK, CHUNK), 1)
    nc = tb // CHUNK
    rid = lax.broadcasted_iota(jnp.int32, (SUB, LANE), 0)
    same_sub = (ci // SUB) == (cj // SUB)
    nsub = CHUNK // SUB
    sel = sel_ref[...]

    for c, h in [(c, h) for c in range(nc) for h in range(GLA_HEADS)]:
        r0 = c * CHUNK
        hk = slice(h * GLA_DK, (h + 1) * GLA_DK)
        qc = q_ref[h, r0:r0 + CHUNK, :] * (GLA_DK ** -0.5)
        kc = k_ref[h, r0:r0 + CHUNK, :]
        bc = b_ref[r0:r0 + CHUNK, hk]
        vc = jnp.concatenate([v_ref[2 * h, r0:r0 + CHUNK, :], v_ref[2 * h + 1, r0:r0 + CHUNK, :]], axis=1)

        for sb in range(nsub):
            qs = qc[sb * SUB:(sb + 1) * SUB]
            bs = bc[sb * SUB:(sb + 1) * SUB]
            for j in range(SUB):
                r = r0 + sb * SUB + j
                e = jnp.exp(jnp.where(rid >= j, bs - b_ref[r:r + 1, hk], -jnp.inf))
                p_ref[h, sb * SUB:(sb + 1) * SUB, j * LANE:(j + 1) * LANE] = (
                    qs * k_ref[h, r:r + 1, :] * e).astype(BF16)
        diag = jnp.dot(p_ref[h], sel, preferred_element_type=F32)
        scores = jnp.where(same_sub, diag, 0.0)

        qparts, kparts = [], []
        for sb in range(1, nsub):
            lo = sb * SUB
            bref = bc[lo - 1:lo, :]
            qt = qc[lo:lo + SUB] * jnp.exp(bc[lo:lo + SUB] - bref)
            kt = kc[:lo] * jnp.exp(bref - bc[:lo])
            qparts.append(jnp.concatenate(
                [jnp.zeros((lo, LANE), F32), qt] +
                ([jnp.zeros((CHUNK - lo - SUB, LANE), F32)] if CHUNK - lo - SUB else []), axis=0))
            kparts.append(jnp.concatenate([kt, jnp.zeros((CHUNK - lo, LANE), F32)], axis=0))
        scores = scores + _mm_nt(jnp.concatenate(qparts, axis=1), jnp.concatenate(kparts, axis=1))

        st = st_ref[h]
        o = _mm(scores, vc) + _mm_nt(qc * jnp.exp(bc), st)
        b_last = bc[CHUNK - 1:CHUNK, :]
        st_ref[h] = st * jnp.exp(b_last) + _mm_tn(vc, kc * jnp.exp(b_last - bc))

        cs = slice(h * GLA_DV, (h + 1) * GLA_DV)
        y = o * lax.rsqrt(jnp.mean(o * o, axis=-1, keepdims=True) + EPS) * ng_ref[:, cs]
        gate = jnp.concatenate([g_ref[2 * h, r0:r0 + CHUNK, :], g_ref[2 * h + 1, r0:r0 + CHUNK, :]], axis=1)
        o_ref[r0:r0 + CHUNK, cs] = (y * _silu(gate)).astype(o_ref.dtype)


def _gla(proj, gate_b, ng, sel, *, b, t, tb):
    m = b * t
    nt = t // tb
    row = lambda bb, tt: bb * nt + tt
    nh, nv = GLA_HEADS, GLA_HEADS * GLA_DV // LANE
    return pl.pallas_call(
        functools.partial(_gla_kernel, tb=tb),
        grid=(b, nt),
        in_specs=[
            pl.BlockSpec((nh, tb, LANE), lambda bb, tt: (CB_GQ // nh, row(bb, tt), 0)),
            pl.BlockSpec((nh, tb, LANE), lambda bb, tt: (CB_GK // nh, row(bb, tt), 0)),
            pl.BlockSpec((nv, tb, LANE), lambda bb, tt: (CB_GV // nv, row(bb, tt), 0)),
            pl.BlockSpec((nv, tb, LANE), lambda bb, tt: (CB_GG // nv, row(bb, tt), 0)),
            pl.BlockSpec((tb, nh * GLA_DK), lambda bb, tt: (row(bb, tt), 0)),
            pl.BlockSpec((1, nh * GLA_DV), lambda bb, tt: (0, 0)),
            pl.BlockSpec((SUB * LANE, CHUNK), lambda bb, tt: (0, 0)),
        ],
        out_specs=pl.BlockSpec((tb, nh * GLA_DV), lambda bb, tt: (row(bb, tt), 0)),
        out_shape=jax.ShapeDtypeStruct((m, nh * GLA_DV), BF16),
        scratch_shapes=[
            pltpu.VMEM((nh, GLA_DV, GLA_DK), F32),
            pltpu.VMEM((nh, CHUNK, SUB * LANE), BF16),
        ],
        compiler_params=pltpu.CompilerParams(
            dimension_semantics=("parallel", "arbitrary"), vmem_limit_bytes=VMEM_LIMIT),
        name="gla",
    )(proj, proj, proj, proj, gate_b, ng, sel)


def _bmm(a, b):
    return jnp.einsum("cij,cjk->cik", a.astype(BF16), b.astype(BF16), preferred_element_type=F32)


def _bmm_nt(a, b):
    return jnp.einsum("cik,cjk->cij", a.astype(BF16), b.astype(BF16), preferred_element_type=F32)


def _dn_kernel(q_ref, k_ref, v_ref, g_ref, parts_ref, wq_ref, wk_ref, wv_ref, ng_ref, pick_ref,
               o_ref, s_ref, ext_ref, wp_ref, np_ref, sb_ref, qp_ref, qu_ref, egl_ref, *, tb, hg):
    nc = tb // CHUNK

    @pl.when(pl.program_id(2) == 0)
    def _():
        s_ref[...] = jnp.zeros_like(s_ref)
        ext_ref[:, 0:8, :] = jnp.zeros((3 * hg, 8, LANE), F32)

    bg = jnp.dot(parts_ref[...], pick_ref[...], preferred_element_type=F32)

    ci = lax.broadcasted_iota(jnp.int32, (CHUNK, CHUNK), 0)
    cj = lax.broadcasted_iota(jnp.int32, (CHUNK, CHUNK), 1)
    lower = (ci >= cj)[None]
    strict = (ci > cj)[None]
    eye = jnp.where(ci == cj, 1.0, 0.0).astype(F32)[None]

    def conv(i, raw, w):
        ext_ref[i, 8:8 + tb, :] = raw
        out = ext_ref[i, 8 - (CONV_K - 1):8 - (CONV_K - 1) + tb, :] * w[0:1, :]
        for tap in range(1, CONV_K):
            lo = 8 - (CONV_K - 1) + tap
            out = out + ext_ref[i, lo:lo + tb, :] * w[tap:tap + 1, :]
        ext_ref[i, 0:8, :] = ext_ref[i, tb:tb + 8, :]
        return _silu(out)

    for hh in range(hg):
        cs = slice(hh * LANE, (hh + 1) * LANE)
        q = conv(3 * hh, q_ref[hh], wq_ref[:, cs])
        k = conv(3 * hh + 1, k_ref[hh], wk_ref[:, cs])
        v = conv(3 * hh + 2, v_ref[hh], wv_ref[:, cs])
        q = q * (lax.rsqrt(jnp.sum(q * q, axis=-1, keepdims=True) + EPS) * (DN_DK ** -0.5))
        k = k * lax.rsqrt(jnp.sum(k * k, axis=-1, keepdims=True) + EPS)
        beta = bg[:, 2 * hh * LANE:(2 * hh + 1) * LANE]
        gc = bg[:, (2 * hh + 1) * LANE:(2 * hh + 2) * LANE]
        egc = jnp.exp(gc)

        gc3 = gc.reshape(nc, CHUNK, LANE)
        k3 = k.reshape(nc, CHUNK, LANE)
        q3 = q.reshape(nc, CHUNK, LANE)
        diff = gc3[:, :, :CHUNK] - jnp.swapaxes(gc3, 1, 2)[:, :CHUNK, :]
        decay = jnp.exp(jnp.where(lower, diff, -jnp.inf))
        a = jnp.where(strict, beta.reshape(nc, CHUNK, LANE)[:, :, :CHUNK] * _bmm_nt(k3, k3) * decay, 0.0)
        tinv = eye - a
        x = _bmm(a, a)
        for _ in range(4):
            tinv = tinv + _bmm(tinv, x)
            x = _bmm(x, x)
        tinv = tinv + _bmm(tinv, x)
        rhs = jnp.concatenate([v * beta, k * (beta * egc)], axis=1).reshape(nc, CHUNK, 2 * LANE)
        wu = _bmm(tinv, rhs).astype(BF16)
        g_last = gc3[:, CHUNK - 1:CHUNK, :]
        ke = (k3 * jnp.exp(g_last - gc3)).astype(BF16)
        qk = (_bmm_nt(q3, k3) * decay).astype(BF16)
        kt = jnp.einsum("cjk,cjd->ckd", ke, wu, preferred_element_type=F32)
        np_ref[hh] = kt[:, :, :DN_DV]
        wp_ref[hh] = kt[:, :, DN_DV:].astype(BF16)
        qt = jnp.einsum("cij,cjd->cid", qk, wu, preferred_element_type=F32)
        qu_ref[hh] = qt[:, :, :DN_DV]
        qp_ref[hh] = ((q * egc).reshape(nc, CHUNK, LANE) - qt[:, :, DN_DV:]).astype(BF16)
        egl_ref[hh] = jnp.exp(g_last)

    def chunk_step(c, carry):
        for hh in range(hg):
            s = s_ref[hh]
            sb = s.astype(BF16)
            sb_ref[hh, c] = sb
            s_ref[hh] = egl_ref[hh, c] * s + np_ref[hh, c] - jnp.dot(wp_ref[hh, c], sb, preferred_element_type=F32)
        return carry

    lax.fori_loop(0, nc, chunk_step, 0)

    for hh in range(hg):
        cs = slice(hh * LANE, (hh + 1) * LANE)
        o = (jnp.einsum("cik,ckd->cid", qp_ref[hh], sb_ref[hh], preferred_element_type=F32)
             + qu_ref[hh]).reshape(tb, LANE)
        y = o * lax.rsqrt(jnp.mean(o * o, axis=-1, keepdims=True) + EPS) * ng_ref[:, cs]
        o_ref[:, cs] = (y * _silu(g_ref[hh])).astype(o_ref.dtype)


def _deltanet(proj, gate_parts, conv_w, pick, ng, *, b, t, tb, hg):
    m = b * t
    nt = t // tb
    ngrp = DN_HEADS // hg
    row = lambda bb, gg, tt: bb * nt + tt
    nc = tb // CHUNK
    return pl.pallas_call(
        functools.partial(_dn_kernel, tb=tb, hg=hg),
        grid=(b, ngrp, nt),
        in_specs=[
            pl.BlockSpec((hg, tb, LANE), lambda bb, gg, tt: (CB_DQ // hg + gg, row(bb, gg, tt), 0)),
            pl.BlockSpec((hg, tb, LANE), lambda bb, gg, tt: (CB_DK // hg + gg, row(bb, gg, tt), 0)),
            pl.BlockSpec((hg, tb, LANE), lambda bb, gg, tt: (CB_DV // hg + gg, row(bb, gg, tt), 0)),
            pl.BlockSpec((hg, tb, LANE), lambda bb, gg, tt: (CB_DG // hg + gg, row(bb, gg, tt), 0)),
            pl.BlockSpec((tb, 3 * LANE), lambda bb, gg, tt: (row(bb, gg, tt), 0)),
            pl.BlockSpec((CONV_K, hg * LANE), lambda bb, gg, tt: (0, gg)),
            pl.BlockSpec((CONV_K, hg * LANE), lambda bb, gg, tt: (0, ngrp + gg)),
            pl.BlockSpec((CONV_K, hg * LANE), lambda bb, gg, tt: (0, 2 * ngrp + gg)),
            pl.BlockSpec((1, hg * DN_DV), lambda bb, gg, tt: (0, gg)),
            pl.BlockSpec((None, 3 * LANE, hg * 2 * LANE), lambda bb, gg, tt: (gg, 0, 0)),
        ],
        out_specs=pl.BlockSpec((tb, hg * DN_DV), lambda bb, gg, tt: (row(bb, gg, tt), gg)),
        out_shape=jax.ShapeDtypeStruct((m, DN_HEADS * DN_DV), BF16),
        scratch_shapes=[
            pltpu.VMEM((hg, DN_DK, DN_DV), F32),
            pltpu.VMEM((3 * hg, tb + 8, LANE), F32),
            pltpu.VMEM((hg, nc, DN_DK, DN_DK), BF16),
            pltpu.VMEM((hg, nc, DN_DK, DN_DV), F32),
            pltpu.VMEM((hg, nc, DN_DK, DN_DV), BF16),
            pltpu.VMEM((hg, nc, CHUNK, DN_DK), BF16),
            pltpu.VMEM((hg, nc, CHUNK, DN_DV), F32),
            pltpu.VMEM((hg, nc, 1, LANE), F32),
        ],
        compiler_params=pltpu.CompilerParams(
            dimension_semantics=("parallel", "parallel", "arbitrary"), vmem_limit_bytes=VMEM_LIMIT),
        name="deltanet",
    )(proj, proj, proj, proj, gate_parts, conv_w, conv_w, conv_w, ng, pick)


def _dn_pick_table(hg):
    ngrp = DN_HEADS // hg
    src = np.arange(3 * LANE) % LANE
    col = np.arange(hg * 2 * LANE)
    hh, is_g = col // (2 * LANE), (col // LANE) % 2
    table = np.zeros((ngrp, 3 * LANE, hg * 2 * LANE), np.float32)
    for grp in range(ngrp):
        want = np.where(is_g == 1, SM_A, SM_BETA) + grp * hg + hh
        table[grp] = src[:, None] == want[None, :]
    return table


def _outproj_kernel(ret_ref, gla_ref, dn_ref, w_ref, x_ref, g_ref, b_ref, o_ref, *maybe_ob_ref,
                    alpha, nk, k_gla, k_dn):
    kidx = pl.program_id(1)

    cols = 512

    def accumulate(mix_ref, first):
        for n0 in range(0, o_ref.shape[1], cols):
            p = jnp.dot(mix_ref[...], w_ref[:, n0:n0 + cols], preferred_element_type=F32)
            if first:
                o_ref[:, n0:n0 + cols] = p
            else:
                o_ref[:, n0:n0 + cols] += p

    pl.when(kidx == 0)(lambda: accumulate(ret_ref, True))
    pl.when((kidx > 0) & (kidx < k_gla))(lambda: accumulate(ret_ref, False))
    pl.when((kidx >= k_gla) & (kidx < k_dn))(lambda: accumulate(gla_ref, False))
    pl.when(kidx >= k_dn)(lambda: accumulate(dn_ref, False))

    @pl.when(kidx == nk - 1)
    def _():
        rows = 16
        for r0 in range(0, o_ref.shape[0], rows):
            r = alpha * x_ref[r0:r0 + rows, :] + o_ref[r0:r0 + rows, :]
            mu = jnp.mean(r, axis=-1, keepdims=True)
            d = r - mu
            var = jnp.mean(d * d, axis=-1, keepdims=True)
            y = d * lax.rsqrt(var + LN_EPS) * g_ref[...] + b_ref[...]
            o_ref[r0:r0 + rows, :] = y
            for ob_ref in maybe_ob_ref:
                ob_ref[r0:r0 + rows, :] = y.astype(BF16)


def _outproj_ln(ret, gla, dn, wout, layer, x, ln_g, ln_b, *, alpha, tm, tk, emit_bf16):
    m, d = x.shape
    k_gla = ret.shape[1] // tk
    k_dn = k_gla + gla.shape[1] // tk
    nk = k_dn + dn.shape[1] // tk
    row_block = pl.BlockSpec((tm, d), lambda i, kk: (i, 0))
    return pl.pallas_call(
        functools.partial(_outproj_kernel, alpha=alpha, nk=nk, k_gla=k_gla, k_dn=k_dn),
        grid=(m // tm, nk),
        in_specs=[
            pl.BlockSpec((tm, tk), lambda i, kk: (i, jnp.minimum(kk, k_gla - 1))),
            pl.BlockSpec((tm, tk), lambda i, kk: (i, jnp.clip(kk - k_gla, 0, k_dn - k_gla - 1))),
            pl.BlockSpec((tm, tk), lambda i, kk: (i, jnp.clip(kk - k_dn, 0, nk - k_dn - 1))),
            pl.BlockSpec((None, tk, d), lambda i, kk: (layer, kk, 0)),
            row_block,
            pl.BlockSpec((1, d), lambda i, kk: (0, 0)),
            pl.BlockSpec((1, d), lambda i, kk: (0, 0)),
        ],
        out_specs=[row_block] + [row_block] * emit_bf16,
        out_shape=[jax.ShapeDtypeStruct((m, d), F32)] + [jax.ShapeDtypeStruct((m, d), BF16)] * emit_bf16,
        compiler_params=pltpu.CompilerParams(
            dimension_semantics=("parallel", "arbitrary"), vmem_limit_bytes=VMEM_LIMIT),
        name="outproj_ln",
    )(ret, gla, dn, wout, x, ln_g, ln_b)


def _pick(n, prefs):
    for p in prefs:
        if n % p == 0:
            return p
    raise ValueError(f"no tile of {prefs} divides {n}")


def _layer(x, xb, pos_col, consts, layer, wcat, wsmall, gla_w_a2, gla_b_a, dn_conv_w, dn_a_log, dn_dt_bias,
           ret_norm_g, gla_norm_g, dn_norm_g, w_out, ln_g, ln_b, *, b, t, alpha, emit_bf16):
    inv2, lg, sel, pick = consts
    m, d = x.shape
    w2pad = jnp.concatenate(
        [gla_w_a2, jnp.zeros((LANE - GLA_RANK, GLA_HEADS * GLA_DK), gla_w_a2.dtype)], axis=0)

    proj, small = _inproj(xb, wcat, wsmall, layer, tm=_pick(m, (1024, 512, 256)), tn=_pick(N_CB * LANE, (1024,)))
    lane_pad = lambda p: jnp.zeros((1, LANE), F32).at[0, SM_A:SM_A + DN_HEADS].set(p)
    gate_b, gate_parts = _gates(small, w2pad, gla_b_a.reshape(1, -1), lane_pad(dn_a_log), lane_pad(dn_dt_bias),
                                tb=_pick(t, (1024, 512, 256)))
    ret = _retention(proj, pos_col, inv2, lg, ret_norm_g.reshape(1, -1), b=b, t=t, tb=_pick(t, (256,)))
    gla = _gla(proj, gate_b, gla_norm_g.reshape(1, -1), sel, b=b, t=t, tb=_pick(t, (512, 256)))
    dn = _deltanet(proj, gate_parts, dn_conv_w, pick, dn_norm_g.reshape(1, -1),
                   b=b, t=t, tb=_pick(t, (1024, 512, 256)), hg=DN_GROUP)
    return _outproj_ln(ret, gla, dn, w_out, layer, x, ln_g.reshape(1, -1), ln_b.reshape(1, -1),
                       alpha=alpha, tm=_pick(m, (512, 256)), tk=512, emit_bf16=emit_bf16)


def kernel(x, positions, w_in, gla_w_a2, gla_b_a, dn_conv_w, dn_a_log, dn_dt_bias, ret_norm_g, gla_norm_g,
           dn_norm_g, w_out, ln_g, ln_b):
    b, t, d = x.shape
    depth = w_in.shape[0]
    alpha = float((2 * depth) ** 0.25)
    half = RET_DK // 2
    inv = ROPE_BASE ** (-jnp.arange(half, dtype=F32) / half)
    inv2 = jnp.concatenate([inv, inv]).reshape(1, LANE)
    lg = jnp.log(1.0 - jnp.power(2.0, -5.0 - jnp.arange(RET_HEADS, dtype=F32)))
    sel = (np.arange(SUB * LANE)[:, None] // LANE == np.arange(CHUNK)[None, :] % SUB)
    consts = (inv2, lg, jnp.asarray(sel, BF16), jnp.asarray(_dn_pick_table(DN_GROUP), BF16))
    pos_col = positions.reshape(b * t, 1)
    h = x.reshape(b * t, d)
    hb = h.astype(BF16)
    wcat, wsmall = _wprep(jnp.swapaxes(w_in, 1, 2), tk=_pick(d, (2048, 1024, 512, 256)),
                          tn=_pick(N_CB_A * LANE, (1024,)))
    w_out_b = w_out.astype(BF16)
    for l in range(depth):
        h, *hb = _layer(h, hb, pos_col, consts, l, wcat, wsmall, gla_w_a2[l], gla_b_a[l], dn_conv_w[l], dn_a_log[l],
                        dn_dt_bias[l], ret_norm_g[l], gla_norm_g[l], dn_norm_g[l], w_out_b, ln_g[l], ln_b[l],
                        b=b, t=t, alpha=alpha, emit_bf16=l + 1 < depth)
        hb = hb[0] if hb else None
    return h.reshape(b, t, d)
```

```python
import functools

import jax
import jax.numpy as jnp
import numpy as np
from jax import lax
from jax.experimental import pallas as pl
from jax.experimental.pallas import tpu as pltpu

CHUNK = 64
RET_HEADS, RET_DK, RET_DV = 4, 128, 256
GLA_HEADS, GLA_DK, GLA_DV, GLA_RANK, GLA_TAU = 4, 128, 256, 16, 16.0
DN_HEADS, DN_DK, DN_DV, CONV_K = 16, 128, 128, 4
ROPE_BASE = 10000.0
EPS = 1e-6
LN_EPS = 1e-5

LANE = 128
SUB = 16
DN_GROUP = 4

CB_RQ, CB_RK, CB_RV, CB_RG = 0, 4, 8, 16
CB_GQ, CB_GK, CB_GV, CB_GG = 24, 28, 32, 40
CB_DQ, CB_DK, CB_DV, CB_DG = 48, 64, 80, 96
N_CB_A = 48
N_CB = 112
SM_BETA, SM_A = 16, 32

VMEM_LIMIT = 60 * 1024 * 1024

F32 = jnp.float32
BF16 = jnp.bfloat16
HI = lax.Precision.HIGHEST


def _mm(a, b):
    return jnp.dot(a.astype(BF16), b.astype(BF16), preferred_element_type=F32)


def _mm_nt(a, b):
    return lax.dot_general(a.astype(BF16), b.astype(BF16), (((1,), (1,)), ((), ())),
                           preferred_element_type=F32)


def _mm_tn(a, b):
    return lax.dot_general(a.astype(BF16), b.astype(BF16), (((0,), (0,)), ((), ())),
                           preferred_element_type=F32)


def _silu(x):
    return x * (1.0 / (1.0 + jnp.exp(-x)))


def _softplus(x):
    return jnp.maximum(x, 0.0) + jnp.log1p(jnp.exp(-jnp.abs(x)))


def _bf16_parts(x):
    hi = x.astype(BF16)
    r = x - hi.astype(F32)
    mid = r.astype(BF16)
    lo = (r - mid.astype(F32)).astype(BF16)
    return hi, mid, lo


def _inproj_kernel(x_ref, w_ref, ws_ref, p_ref, s_ref, *, tn):
    x = x_ref[...]
    acc = jnp.dot(x, w_ref[...], preferred_element_type=F32)
    for c in range(tn // LANE):
        p_ref[c] = acc[:, c * LANE:(c + 1) * LANE]

    @pl.when(pl.program_id(1) == 0)
    def _():
        s_ref[...] = jnp.dot(x, ws_ref[...], preferred_element_type=F32)


def _wprep_kernel(a_ref, ga_ref, bd_ref, o_ref, s_ref):
    o_ref[...] = a_ref[0].T.astype(BF16)

    @pl.when(pl.program_id(2) == 0)
    def _():
        pad = jnp.zeros((LANE - GLA_RANK - 2 * DN_HEADS, ga_ref.shape[2]), F32)
        s_ref[...] = jnp.concatenate([ga_ref[0], bd_ref[0], pad], axis=0).T.astype(BF16)


def _wprep(w_t, *, tk, tn):
    depth, _, d = w_t.shape
    n_a = N_CB_A * LANE
    dn0 = n_a + GLA_RANK
    small0 = dn0 + (N_CB - N_CB_A) * LANE
    row0 = lambda j: (j * (tn // GLA_RANK) + jnp.minimum(j // (n_a // tn), 1)) * GLA_RANK
    return pl.pallas_call(
        _wprep_kernel,
        grid=(depth, d // tk, N_CB * LANE // tn),
        in_specs=[
            pl.BlockSpec((pl.Element(1), pl.Element(tn), pl.Element(tk)), lambda l, kk, j: (l, row0(j), kk * tk)),
            pl.BlockSpec((pl.Element(1), pl.Element(GLA_RANK), pl.Element(tk)), lambda l, kk, j: (l, n_a, kk * tk)),
            pl.BlockSpec((pl.Element(1), pl.Element(2 * DN_HEADS), pl.Element(tk)),
                         lambda l, kk, j: (l, small0, kk * tk)),
        ],
        out_specs=[
            pl.BlockSpec((None, tk, tn), lambda l, kk, j: (l, kk, j)),
            pl.BlockSpec((None, tk, LANE), lambda l, kk, j: (l, kk, 0)),
        ],
        out_shape=[
            jax.ShapeDtypeStruct((depth, d, N_CB * LANE), BF16),
            jax.ShapeDtypeStruct((depth, d, LANE), BF16),
        ],
        compiler_params=pltpu.CompilerParams(
            dimension_semantics=("parallel", "parallel", "arbitrary"), vmem_limit_bytes=VMEM_LIMIT),
        name="wprep",
    )(w_t, w_t, w_t)


def _inproj(xb, wcat, wsmall, layer, *, tm, tn):
    m, d = xb.shape
    n = wcat.shape[2]
    grid = (m // tm, n // tn)
    return pl.pallas_call(
        functools.partial(_inproj_kernel, tn=tn),
        grid=grid,
        in_specs=[
            pl.BlockSpec((tm, d), lambda i, j: (i, 0)),
            pl.BlockSpec((None, d, tn), lambda i, j: (layer, 0, j)),
            pl.BlockSpec((None, d, LANE), lambda i, j: (layer, 0, 0)),
        ],
        out_specs=[
            pl.BlockSpec((tn // LANE, tm, LANE), lambda i, j: (j, i, 0)),
            pl.BlockSpec((tm, LANE), lambda i, j: (i, 0)),
        ],
        out_shape=[
            jax.ShapeDtypeStruct((n // LANE, m, LANE), F32),
            jax.ShapeDtypeStruct((m, LANE), F32),
        ],
        compiler_params=pltpu.CompilerParams(
            dimension_semantics=("parallel", "arbitrary"), vmem_limit_bytes=VMEM_LIMIT),
        name="inproj",
    )(xb, wcat, wsmall)


def _gates_kernel(sm_ref, w2_ref, ba_ref, alog_ref, dtb_ref, b_ref, parts_ref, *, tb):
    nc = tb // CHUNK
    ci = lax.broadcasted_iota(jnp.int32, (CHUNK, CHUNK), 0)
    cj = lax.broadcasted_iota(jnp.int32, (CHUNK, CHUNK), 1)
    tri = jnp.broadcast_to(jnp.where(ci >= cj, 1.0, 0.0).astype(BF16)[None], (nc, CHUNK, CHUNK))

    def chunk_cumsum(v):
        n = v.shape[1]
        return sum(jnp.einsum("cij,cjk->cik", tri, p.reshape(nc, CHUNK, n), preferred_element_type=F32)
                   for p in _bf16_parts(v)).reshape(tb, n)

    sm = sm_ref[...]
    z = jnp.dot(sm, w2_ref[...], precision=HI, preferred_element_type=F32) + ba_ref[...]
    b_ref[...] = chunk_cumsum((jnp.minimum(z, 0.0) - jnp.log1p(jnp.exp(-jnp.abs(z)))) / GLA_TAU)
    lane = lax.broadcasted_iota(jnp.int32, (1, LANE), 1)
    beta_all = 1.0 / (1.0 + jnp.exp(-sm))
    gc_all = chunk_cumsum(-jnp.exp(alog_ref[...]) * _softplus(sm + dtb_ref[...]))
    parts_ref[...] = jnp.concatenate(_bf16_parts(jnp.where(lane < SM_A, beta_all, gc_all)), axis=1)


def _gates(small, w2pad, ba, alog_row, dtb_row, *, tb):
    m = small.shape[0]
    n_gla = w2pad.shape[1]
    whole = lambda shape: pl.BlockSpec(shape, lambda i: (0, 0))
    return pl.pallas_call(
        functools.partial(_gates_kernel, tb=tb),
        grid=(m // tb,),
        in_specs=[
            pl.BlockSpec((tb, LANE), lambda i: (i, 0)),
            whole((LANE, n_gla)), whole((1, n_gla)), whole((1, LANE)), whole((1, LANE)),
        ],
        out_specs=[
            pl.BlockSpec((tb, n_gla), lambda i: (i, 0)),
            pl.BlockSpec((tb, 3 * LANE), lambda i: (i, 0)),
        ],
        out_shape=[
            jax.ShapeDtypeStruct((m, n_gla), F32),
            jax.ShapeDtypeStruct((m, 3 * LANE), BF16),
        ],
        compiler_params=pltpu.CompilerParams(
            dimension_semantics=("parallel",), vmem_limit_bytes=VMEM_LIMIT),
        name="gates",
    )(small, w2pad, ba, alog_row, dtb_row)


def _rope_kernel(pos_ref, inv_ref, cos_ref, sin_ref):
    ang = pos_ref[...].astype(F32) * inv_ref[...]
    lane = lax.broadcasted_iota(jnp.int32, (1, LANE), 1)
    cos_ref[...] = jnp.cos(ang)
    sin_ref[...] = jnp.sin(ang) * jnp.where(lane < LANE // 2, -1.0, 1.0)


def _rope(pos_col, inv2, *, tb):
    m = pos_col.shape[0]
    table = pl.BlockSpec((tb, LANE), lambda i: (i, 0))
    return pl.pallas_call(
        _rope_kernel,
        grid=(m // tb,),
        in_specs=[pl.BlockSpec((tb, 1), lambda i: (i, 0)), pl.BlockSpec((1, LANE), lambda i: (0, 0))],
        out_specs=[table, table],
        out_shape=[jax.ShapeDtypeStruct((m, LANE), F32)] * 2,
        compiler_params=pltpu.CompilerParams(dimension_semantics=("parallel",), vmem_limit_bytes=VMEM_LIMIT),
        name="rope",
    )(pos_col, inv2)


def _ret_kernel(lg_ref, q_ref, k_ref, v_ref, g_ref, cos_ref, sin_ref, ng_ref, o_ref, s_ref, *, tb):
    @pl.when(pl.program_id(1) == 0)
    def _():
        s_ref[...] = jnp.zeros_like(s_ref)

    cosv = cos_ref[...]
    sinv = sin_ref[...]

    def rot(t):
        return t * cosv + pltpu.roll(t, LANE // 2, 1) * sinv

    ii = lax.broadcasted_iota(jnp.int32, (tb, tb), 0)
    jj = lax.broadcasted_iota(jnp.int32, (tb, tb), 1)
    causal = ii >= jj
    dist = jnp.where(causal, (ii - jj).astype(F32), 0.0)
    ri = lax.broadcasted_iota(jnp.int32, (tb, RET_DV), 0).astype(F32)

    for h in range(RET_HEADS):
        lg = lg_ref[h]
        q = rot(q_ref[h]) * (RET_DK ** -0.5)
        k = rot(k_ref[h])
        v = jnp.concatenate([v_ref[2 * h], v_ref[2 * h + 1]], axis=1)
        dmask = jnp.where(causal, jnp.exp(dist * lg), 0.0)
        xi = jnp.exp((ri + 1.0) * lg)
        zeta = jnp.exp((tb - 1.0 - ri[:, :RET_DK]) * lg)
        g_chunk = jnp.exp(jnp.full((1, RET_DV), tb * lg, F32))

        s = s_ref[h]
        scores = _mm_nt(q, k) * dmask
        o = _mm(scores, v) + _mm(q, s) * xi
        s_ref[h] = g_chunk * s + _mm_tn(k * zeta, v)

        mu = jnp.mean(o, axis=-1, keepdims=True)
        d = o - mu
        var = jnp.mean(d * d, axis=-1, keepdims=True)
        cs = slice(h * RET_DV, (h + 1) * RET_DV)
        y = d * lax.rsqrt(var + EPS) * ng_ref[:, cs]
        gate = jnp.concatenate([g_ref[2 * h], g_ref[2 * h + 1]], axis=1)
        o_ref[:, cs] = (y * _silu(gate)).astype(o_ref.dtype)


def _retention(proj, rope_cos, rope_sin, lg, ng, *, b, t, tb):
    m = b * t
    nt = t // tb
    row = lambda bb, tt: bb * nt + tt
    nh, nv = RET_HEADS, RET_HEADS * RET_DV // LANE
    return pl.pallas_call(
        functools.partial(_ret_kernel, tb=tb),
        grid=(b, nt),
        in_specs=[
            pl.BlockSpec(memory_space=pltpu.SMEM),
            pl.BlockSpec((nh, tb, LANE), lambda bb, tt: (CB_RQ // nh, row(bb, tt), 0)),
            pl.BlockSpec((nh, tb, LANE), lambda bb, tt: (CB_RK // nh, row(bb, tt), 0)),
            pl.BlockSpec((nv, tb, LANE), lambda bb, tt: (CB_RV // nv, row(bb, tt), 0)),
            pl.BlockSpec((nv, tb, LANE), lambda bb, tt: (CB_RG // nv, row(bb, tt), 0)),
            pl.BlockSpec((tb, LANE), lambda bb, tt: (row(bb, tt), 0)),
            pl.BlockSpec((tb, LANE), lambda bb, tt: (row(bb, tt), 0)),
            pl.BlockSpec((1, nh * RET_DV), lambda bb, tt: (0, 0)),
        ],
        out_specs=pl.BlockSpec((tb, nh * RET_DV), lambda bb, tt: (row(bb, tt), 0)),
        out_shape=jax.ShapeDtypeStruct((m, nh * RET_DV), BF16),
        scratch_shapes=[pltpu.VMEM((nh, RET_DK, RET_DV), F32)],
        compiler_params=pltpu.CompilerParams(
            dimension_semantics=("parallel", "arbitrary"), vmem_limit_bytes=VMEM_LIMIT),
        name="retention",
    )(lg, proj, proj, proj, proj, rope_cos, rope_sin, ng)


def _gla_kernel(q_ref, k_ref, v_ref, g_ref, b_ref, ng_ref, sel_ref, o_ref, st_ref, p_ref, *, tb):
    @pl.when(pl.program_id(1) == 0)
    def _():
        st_ref[...] = jnp.zeros_like(st_ref)

    ci = lax.broadcasted_iota(jnp.int32, (CHUNK, CHUNK), 0)
    cj = lax.broadcasted_iota(jnp.int32, (CHUNK, CHUNK), 1)
    nc = tb // CHUNK
    rid = lax.broadcasted_iota(jnp.int32, (SUB, LANE), 0)
    same_sub = (ci // SUB) == (cj // SUB)
    nsub = CHUNK // SUB
    sel = sel_ref[...]

    for c, h in [(c, h) for c in range(nc) for h in range(GLA_HEADS)]:
        r0 = c * CHUNK
        hk = slice(h * GLA_DK, (h + 1) * GLA_DK)
        qc = q_ref[h, r0:r0 + CHUNK, :] * (GLA_DK ** -0.5)
        kc = k_ref[h, r0:r0 + CHUNK, :]
        bc = b_ref[r0:r0 + CHUNK, hk]
        vc = jnp.concatenate([v_ref[2 * h, r0:r0 + CHUNK, :], v_ref[2 * h + 1, r0:r0 + CHUNK, :]], axis=1)

        for sb in range(nsub):
            qs = qc[sb * SUB:(sb + 1) * SUB]
            bs = bc[sb * SUB:(sb + 1) * SUB]
            for j in range(SUB):
                r = r0 + sb * SUB + j
                e = jnp.exp(jnp.where(rid >= j, bs - b_ref[r:r + 1, hk], -jnp.inf))
                p_ref[h, sb * SUB:(sb + 1) * SUB, j * LANE:(j + 1) * LANE] = (
                    qs * k_ref[h, r:r + 1, :] * e).astype(BF16)
        diag = jnp.dot(p_ref[h], sel, preferred_element_type=F32)
        scores = jnp.where(same_sub, diag, 0.0)

        qparts, kparts = [], []
        for sb in range(1, nsub):
            lo = sb * SUB
            bref = bc[lo - 1:lo, :]
            qt = qc[lo:lo + SUB] * jnp.exp(bc[lo:lo + SUB] - bref)
            kt = kc[:lo] * jnp.exp(bref - bc[:lo])
            qparts.append(jnp.concatenate(
                [jnp.zeros((lo, LANE), F32), qt] +
                ([jnp.zeros((CHUNK - lo - SUB, LANE), F32)] if CHUNK - lo - SUB else []), axis=0))
            kparts.append(jnp.concatenate([kt, jnp.zeros((CHUNK - lo, LANE), F32)], axis=0))
        scores = scores + _mm_nt(jnp.concatenate(qparts, axis=1), jnp.concatenate(kparts, axis=1))

        st = st_ref[h]
        o = _mm(scores, vc) + _mm_nt(qc * jnp.exp(bc), st)
        b_last = bc[CHUNK - 1:CHUNK, :]
        st_ref[h] = st * jnp.exp(b_last) + _mm_tn(vc, kc * jnp.exp(b_last - bc))

        cs = slice(h * GLA_DV, (h + 1) * GLA_DV)
        y = o * lax.rsqrt(jnp.mean(o * o, axis=-1, keepdims=True) + EPS) * ng_ref[:, cs]
        gate = jnp.concatenate([g_ref[2 * h, r0:r0 + CHUNK, :], g_ref[2 * h + 1, r0:r0 + CHUNK, :]], axis=1)
        o_ref[r0:r0 + CHUNK, cs] = (y * _silu(gate)).astype(o_ref.dtype)


def _gla(proj, gate_b, ng, sel, *, b, t, tb):
    m = b * t
    nt = t // tb
    row = lambda bb, tt: bb * nt + tt
    nh, nv = GLA_HEADS, GLA_HEADS * GLA_DV // LANE
    return pl.pallas_call(
        functools.partial(_gla_kernel, tb=tb),
        grid=(b, nt),
        in_specs=[
            pl.BlockSpec((nh, tb, LANE), lambda bb, tt: (CB_GQ // nh, row(bb, tt), 0)),
            pl.BlockSpec((nh, tb, LANE), lambda bb, tt: (CB_GK // nh, row(bb, tt), 0)),
            pl.BlockSpec((nv, tb, LANE), lambda bb, tt: (CB_GV // nv, row(bb, tt), 0)),
            pl.BlockSpec((nv, tb, LANE), lambda bb, tt: (CB_GG // nv, row(bb, tt), 0)),
            pl.BlockSpec((tb, nh * GLA_DK), lambda bb, tt: (row(bb, tt), 0)),
            pl.BlockSpec((1, nh * GLA_DV), lambda bb, tt: (0, 0)),
            pl.BlockSpec((SUB * LANE, CHUNK), lambda bb, tt: (0, 0)),
        ],
        out_specs=pl.BlockSpec((tb, nh * GLA_DV), lambda bb, tt: (row(bb, tt), 0)),
        out_shape=jax.ShapeDtypeStruct((m, nh * GLA_DV), BF16),
        scratch_shapes=[
            pltpu.VMEM((nh, GLA_DV, GLA_DK), F32),
            pltpu.VMEM((nh, CHUNK, SUB * LANE), BF16),
        ],
        compiler_params=pltpu.CompilerParams(
            dimension_semantics=("parallel", "arbitrary"), vmem_limit_bytes=VMEM_LIMIT),
        name="gla",
    )(proj, proj, proj, proj, gate_b, ng, sel)


def _bmm(a, b):
    return jnp.einsum("cij,cjk->cik", a.astype(BF16), b.astype(BF16), preferred_element_type=F32)


def _bmm_nt(a, b):
    return jnp.einsum("cik,cjk->cij", a.astype(BF16), b.astype(BF16), preferred_element_type=F32)


def _dn_kernel(q_ref, k_ref, v_ref, g_ref, parts_ref, wq_ref, wk_ref, wv_ref, ng_ref, pick_ref,
               o_ref, s_ref, ext_ref, wp_ref, np_ref, sb_ref, qp_ref, qu_ref, egl_ref, *, tb, hg):
    nc = tb // CHUNK

    @pl.when(pl.program_id(2) == 0)
    def _():
        s_ref[...] = jnp.zeros_like(s_ref)
        ext_ref[:, 0:8, :] = jnp.zeros((3 * hg, 8, LANE), F32)

    bg = jnp.dot(parts_ref[...], pick_ref[...], preferred_element_type=F32)

    ci = lax.broadcasted_iota(jnp.int32, (CHUNK, CHUNK), 0)
    cj = lax.broadcasted_iota(jnp.int32, (CHUNK, CHUNK), 1)
    lower = (ci >= cj)[None]
    strict = (ci > cj)[None]
    eye = jnp.where(ci == cj, 1.0, 0.0).astype(F32)[None]

    def conv(i, raw, w):
        ext_ref[i, 8:8 + tb, :] = raw
        out = ext_ref[i, 8 - (CONV_K - 1):8 - (CONV_K - 1) + tb, :] * w[0:1, :]
        for tap in range(1, CONV_K):
            lo = 8 - (CONV_K - 1) + tap
            out = out + ext_ref[i, lo:lo + tb, :] * w[tap:tap + 1, :]
        ext_ref[i, 0:8, :] = ext_ref[i, tb:tb + 8, :]
        return _silu(out)

    for hh in range(hg):
        cs = slice(hh * LANE, (hh + 1) * LANE)
        q = conv(3 * hh, q_ref[hh], wq_ref[:, cs])
        k = conv(3 * hh + 1, k_ref[hh], wk_ref[:, cs])
        v = conv(3 * hh + 2, v_ref[hh], wv_ref[:, cs])
        q = q * (lax.rsqrt(jnp.sum(q * q, axis=-1, keepdims=True) + EPS) * (DN_DK ** -0.5))
        k = k * lax.rsqrt(jnp.sum(k * k, axis=-1, keepdims=True) + EPS)
        beta = bg[:, 2 * hh * LANE:(2 * hh + 1) * LANE]
        gc = bg[:, (2 * hh + 1) * LANE:(2 * hh + 2) * LANE]
        egc = jnp.exp(gc)

        gc3 = gc.reshape(nc, CHUNK, LANE)
        k3 = k.reshape(nc, CHUNK, LANE)
        q3 = q.reshape(nc, CHUNK, LANE)
        diff = gc3[:, :, :CHUNK] - jnp.swapaxes(gc3, 1, 2)[:, :CHUNK, :]
        decay = jnp.exp(jnp.where(lower, diff, -jnp.inf))
        a = jnp.where(strict, beta.reshape(nc, CHUNK, LANE)[:, :, :CHUNK] * _bmm_nt(k3, k3) * decay, 0.0)
        tinv = eye - a
        x = _bmm(a, a)
        for _ in range(4):
            tinv = tinv + _bmm(tinv, x)
            x = _bmm(x, x)
        tinv = tinv + _bmm(tinv, x)
        rhs = jnp.concatenate([v * beta, k * (beta * egc)], axis=1).reshape(nc, CHUNK, 2 * LANE)
        wu = _bmm(tinv, rhs).astype(BF16)
        g_last = gc3[:, CHUNK - 1:CHUNK, :]
        ke = (k3 * jnp.exp(g_last - gc3)).astype(BF16)
        qk = (_bmm_nt(q3, k3) * decay).astype(BF16)
        kt = jnp.einsum("cjk,cjd->ckd", ke, wu, preferred_element_type=F32)
        np_ref[hh] = kt[:, :, :DN_DV]
        wp_ref[hh] = kt[:, :, DN_DV:].astype(BF16)
        qt = jnp.einsum("cij,cjd->cid", qk, wu, preferred_element_type=F32)
        qu_ref[hh] = qt[:, :, :DN_DV]
        qp_ref[hh] = ((q * egc).reshape(nc, CHUNK, LANE) - qt[:, :, DN_DV:]).astype(BF16)
        egl_ref[hh] = jnp.exp(g_last)

    def chunk_step(c, carry):
        for hh in range(hg):
            s = s_ref[hh]
            sb = s.astype(BF16)
            sb_ref[hh, c] = sb
            s_ref[hh] = egl_ref[hh, c] * s + np_ref[hh, c] - jnp.dot(wp_ref[hh, c], sb, preferred_element_type=F32)
        return carry

    lax.fori_loop(0, nc, chunk_step, 0)

    for hh in range(hg):
        cs = slice(hh * LANE, (hh + 1) * LANE)
        o = (jnp.einsum("cik,ckd->cid", qp_ref[hh], sb_ref[hh], preferred_element_type=F32)
             + qu_ref[hh]).reshape(tb, LANE)
        y = o * lax.rsqrt(jnp.mean(o * o, axis=-1, keepdims=True) + EPS) * ng_ref[:, cs]
        o_ref[:, cs] = (y * _silu(g_ref[hh])).astype(o_ref.dtype)


def _deltanet(proj, gate_parts, conv_w, pick, ng, *, b, t, tb, hg):
    m = b * t
    nt = t // tb
    ngrp = DN_HEADS // hg
    row = lambda bb, gg, tt: bb * nt + tt
    nc = tb // CHUNK
    return pl.pallas_call(
        functools.partial(_dn_kernel, tb=tb, hg=hg),
        grid=(b, ngrp, nt),
        in_specs=[
            pl.BlockSpec((hg, tb, LANE), lambda bb, gg, tt: (CB_DQ // hg + gg, row(bb, gg, tt), 0)),
            pl.BlockSpec((hg, tb, LANE), lambda bb, gg, tt: (CB_DK // hg + gg, row(bb, gg, tt), 0)),
            pl.BlockSpec((hg, tb, LANE), lambda bb, gg, tt: (CB_DV // hg + gg, row(bb, gg, tt), 0)),
            pl.BlockSpec((hg, tb, LANE), lambda bb, gg, tt: (CB_DG // hg + gg, row(bb, gg, tt), 0)),
            pl.BlockSpec((tb, 3 * LANE), lambda bb, gg, tt: (row(bb, gg, tt), 0)),
            pl.BlockSpec((CONV_K, hg * LANE), lambda bb, gg, tt: (0, gg)),
            pl.BlockSpec((CONV_K, hg * LANE), lambda bb, gg, tt: (0, ngrp + gg)),
            pl.BlockSpec((CONV_K, hg * LANE), lambda bb, gg, tt: (0, 2 * ngrp + gg)),
            pl.BlockSpec((1, hg * DN_DV), lambda bb, gg, tt: (0, gg)),
            pl.BlockSpec((None, 3 * LANE, hg * 2 * LANE), lambda bb, gg, tt: (gg, 0, 0)),
        ],
        out_specs=pl.BlockSpec((tb, hg * DN_DV), lambda bb, gg, tt: (row(bb, gg, tt), gg)),
        out_shape=jax.ShapeDtypeStruct((m, DN_HEADS * DN_DV), BF16),
        scratch_shapes=[
            pltpu.VMEM((hg, DN_DK, DN_DV), F32),
            pltpu.VMEM((3 * hg, tb + 8, LANE), F32),
            pltpu.VMEM((hg, nc, DN_DK, DN_DK), BF16),
            pltpu.VMEM((hg, nc, DN_DK, DN_DV), F32),
            pltpu.VMEM((hg, nc, DN_DK, DN_DV), BF16),
            pltpu.VMEM((hg, nc, CHUNK, DN_DK), BF16),
            pltpu.VMEM((hg, nc, CHUNK, DN_DV), F32),
            pltpu.VMEM((hg, nc, 1, LANE), F32),
        ],
        compiler_params=pltpu.CompilerParams(
            dimension_semantics=("parallel", "parallel", "arbitrary"), vmem_limit_bytes=VMEM_LIMIT),
        name="deltanet",
    )(proj, proj, proj, proj, gate_parts, conv_w, conv_w, conv_w, ng, pick)


def _dn_pick_table(hg):
    ngrp = DN_HEADS // hg
    src = np.arange(3 * LANE) % LANE
    col = np.arange(hg * 2 * LANE)
    hh, is_g = col // (2 * LANE), (col // LANE) % 2
    table = np.zeros((ngrp, 3 * LANE, hg * 2 * LANE), np.float32)
    for grp in range(ngrp):
        want = np.where(is_g == 1, SM_A, SM_BETA) + grp * hg + hh
        table[grp] = src[:, None] == want[None, :]
    return table


def _outproj_kernel(ret_ref, gla_ref, dn_ref, w_ref, x_ref, g_ref, b_ref, o_ref, *maybe_ob_ref,
                    alpha, nk, k_gla, k_dn):
    kidx = pl.program_id(1)

    cols = 512

    def accumulate(mix_ref, first):
        for n0 in range(0, o_ref.shape[1], cols):
            p = jnp.dot(mix_ref[...], w_ref[:, n0:n0 + cols], preferred_element_type=F32)
            if first:
                o_ref[:, n0:n0 + cols] = p
            else:
                o_ref[:, n0:n0 + cols] += p

    pl.when(kidx == 0)(lambda: accumulate(ret_ref, True))
    pl.when((kidx > 0) & (kidx < k_gla))(lambda: accumulate(ret_ref, False))
    pl.when((kidx >= k_gla) & (kidx < k_dn))(lambda: accumulate(gla_ref, False))
    pl.when(kidx >= k_dn)(lambda: accumulate(dn_ref, False))

    @pl.when(kidx == nk - 1)
    def _():
        rows = 16
        for r0 in range(0, o_ref.shape[0], rows):
            r = alpha * x_ref[r0:r0 + rows, :] + o_ref[r0:r0 + rows, :]
            mu = jnp.mean(r, axis=-1, keepdims=True)
            d = r - mu
            var = jnp.mean(d * d, axis=-1, keepdims=True)
            y = d * lax.rsqrt(var + LN_EPS) * g_ref[...] + b_ref[...]
            o_ref[r0:r0 + rows, :] = y
            for ob_ref in maybe_ob_ref:
                ob_ref[r0:r0 + rows, :] = y.astype(BF16)


def _outproj_ln(ret, gla, dn, wout, layer, x, ln_g, ln_b, *, alpha, tm, tk, emit_bf16):
    m, d = x.shape
    k_gla = ret.shape[1] // tk
    k_dn = k_gla + gla.shape[1] // tk
    nk = k_dn + dn.shape[1] // tk
    row_block = pl.BlockSpec((tm, d), lambda i, kk: (i, 0))
    return pl.pallas_call(
        functools.partial(_outproj_kernel, alpha=alpha, nk=nk, k_gla=k_gla, k_dn=k_dn),
        grid=(m // tm, nk),
        in_specs=[
            pl.BlockSpec((tm, tk), lambda i, kk: (i, jnp.minimum(kk, k_gla - 1))),
            pl.BlockSpec((tm, tk), lambda i, kk: (i, jnp.clip(kk - k_gla, 0, k_dn - k_gla - 1))),
            pl.BlockSpec((tm, tk), lambda i, kk: (i, jnp.clip(kk - k_dn, 0, nk - k_dn - 1))),
            pl.BlockSpec((None, tk, d), lambda i, kk: (layer, kk, 0)),
            row_block,
            pl.BlockSpec((1, d), lambda i, kk: (0, 0)),
            pl.BlockSpec((1, d), lambda i, kk: (0, 0)),
        ],
        out_specs=[row_block] + [row_block] * emit_bf16,
        out_shape=[jax.ShapeDtypeStruct((m, d), F32)] + [jax.ShapeDtypeStruct((m, d), BF16)] * emit_bf16,
        compiler_params=pltpu.CompilerParams(
            dimension_semantics=("parallel", "arbitrary"), vmem_limit_bytes=VMEM_LIMIT),
        name="outproj_ln",
    )(ret, gla, dn, wout, x, ln_g, ln_b)


def _pick(n, prefs):
    for p in prefs:
        if n % p == 0:
            return p
    raise ValueError(f"no tile of {prefs} divides {n}")


def _layer(x, xb, consts, layer, wcat, wsmall, gla_w_a2, gla_b_a, dn_conv_w, dn_a_log, dn_dt_bias,
           ret_norm_g, gla_norm_g, dn_norm_g, w_out, ln_g, ln_b, *, b, t, alpha, emit_bf16):
    rope_cos, rope_sin, lg, sel, pick = consts
    m, d = x.shape
    w2pad = jnp.concatenate(
        [gla_w_a2, jnp.zeros((LANE - GLA_RANK, GLA_HEADS * GLA_DK), gla_w_a2.dtype)], axis=0)

    proj, small = _inproj(xb, wcat, wsmall, layer, tm=_pick(m, (1024, 512, 256)), tn=_pick(N_CB * LANE, (1024,)))
    lane_pad = lambda p: jnp.zeros((1, LANE), F32).at[0, SM_A:SM_A + DN_HEADS].set(p)
    gate_b, gate_parts = _gates(small, w2pad, gla_b_a.reshape(1, -1), lane_pad(dn_a_log), lane_pad(dn_dt_bias),
                                tb=_pick(t, (1024, 512, 256)))
    ret = _retention(proj, rope_cos, rope_sin, lg, ret_norm_g.reshape(1, -1), b=b, t=t, tb=_pick(t, (256,)))
    gla = _gla(proj, gate_b, gla_norm_g.reshape(1, -1), sel, b=b, t=t, tb=_pick(t, (512, 256)))
    dn = _deltanet(proj, gate_parts, dn_conv_w, pick, dn_norm_g.reshape(1, -1),
                   b=b, t=t, tb=_pick(t, (1024, 512, 256)), hg=DN_GROUP)
    return _outproj_ln(ret, gla, dn, w_out, layer, x, ln_g.reshape(1, -1), ln_b.reshape(1, -1),
                       alpha=alpha, tm=_pick(m, (512, 256)), tk=512, emit_bf16=emit_bf16)


def kernel(x, positions, w_in, gla_w_a2, gla_b_a, dn_conv_w, dn_a_log, dn_dt_bias, ret_norm_g, gla_norm_g,
           dn_norm_g, w_out, ln_g, ln_b):
    b, t, d = x.shape
    depth = w_in.shape[0]
    alpha = float((2 * depth) ** 0.25)
    half = RET_DK // 2
    inv = ROPE_BASE ** (-jnp.arange(half, dtype=F32) / half)
    inv2 = jnp.concatenate([inv, inv]).reshape(1, LANE)
    lg = jnp.log(1.0 - jnp.power(2.0, -5.0 - jnp.arange(RET_HEADS, dtype=F32)))
    sel = (np.arange(SUB * LANE)[:, None] // LANE == np.arange(CHUNK)[None, :] % SUB)
    rope_cos, rope_sin = _rope(positions.reshape(b * t, 1), inv2, tb=_pick(b * t, (1024, 512, 256)))
    consts = (rope_cos, rope_sin, lg, jnp.asarray(sel, BF16), jnp.asarray(_dn_pick_table(DN_GROUP), BF16))
    h = x.reshape(b * t, d)
    hb = h.astype(BF16)
    wcat, wsmall = _wprep(jnp.swapaxes(w_in, 1, 2), tk=_pick(d, (2048, 1024, 512, 256)),
                          tn=_pick(N_CB_A * LANE, (1024,)))
    w_out_b = w_out.astype(BF16)
    for l in range(depth):
        h, *hb = _layer(h, hb, consts, l, wcat, wsmall, gla_w_a2[l], gla_b_a[l], dn_conv_w[l], dn_a_log[l],
                        dn_dt_bias[l], ret_norm_g[l], gla_norm_g[l], dn_norm_g[l], w_out_b, ln_g[l], ln_b[l],
                        b=b, t=t, alpha=alpha, emit_bf16=l + 1 < depth)
        hb = hb[0] if hb else None
    return h.reshape(b, t, d)
```
